```python
import jax, jax.numpy as jnp
from jax import lax
import numpy as np

D_MODEL = 2048
BATCH = 4
SEQ = 2048
DEPTH = 2
DEC_BATCH = 128
DEC_SEQ = 1
PAST_LEN = 2048
PAGE_SIZE = 128

MIX_W = D_MODEL
POOL_W = MIX_W // 2
ATT_W = MIX_W - POOL_W
HEAD_DIM = 128
N_HEADS = ATT_W // HEAD_DIM
POOL_WINDOWS = (2, 4, 8, 16)
POOL_GC = POOL_W // len(POOL_WINDOWS)
POOL_BUF = max(POOL_WINDOWS) - 1
D_FF = 5632
Q_BLOCK = 128
N_MOD = 9
SB_BIAS_INIT = -6.0
EPS = 1e-6

kernel_name = "hymba_pool_stickbreak_macaron_adaln_step"


def rmsnorm(x, g):
    xf = x.astype(jnp.float32)
    y = xf * lax.rsqrt(jnp.mean(xf * xf, axis=-1, keepdims=True) + EPS)
    return (y * g.astype(jnp.float32)).astype(x.dtype)


def modulate(x, g, shift, scale):
    return rmsnorm(x, g) * (1 + scale[:, None, :]) + shift[:, None, :]


def swiglu(h, w1, w3, w2):
    return (jax.nn.silu(h @ w1) * (h @ w3)) @ w2


def pool_mixer(u_ext, pos, n_prefix, w_pool, pool_scale):
    T = pos.shape[0]
    cs = jnp.cumsum(u_ext.astype(jnp.float32), axis=1)
    csp = jnp.pad(cs, ((0, 0), (1, 0), (0, 0)))
    rows = n_prefix + jnp.arange(T)
    hi = cs[:, n_prefix:]
    u_new = u_ext[:, n_prefix:].astype(jnp.float32)
    outs = []
    for g, w in enumerate(POOL_WINDOWS):
        sl = slice(g * POOL_GC, (g + 1) * POOL_GC)
        lo = jnp.take(csp[..., sl], jnp.maximum(rows + 1 - w, 0), axis=1)
        cnt = jnp.minimum(pos + 1, w).astype(jnp.float32)[None, :, None]
        d = ((hi[..., sl] - lo) / cnt - u_new[..., sl]).astype(u_ext.dtype)
        outs.append(jnp.einsum('btc,cd->btd', d, w_pool[g]))
    return jnp.concatenate(outs, axis=-1) * pool_scale


def sb_attend(q, k, v, q_pos, k_pos, sb_bias):
    z = jnp.einsum('bqhd,bkhd->bhqk', q, k).astype(jnp.float32) * (HEAD_DIM ** -0.5)
    z = z + sb_bias.astype(jnp.float32)[None, :, None, None]
    mask = (k_pos[None, :] < q_pos[:, None])[None, None]
    log_fail = jnp.where(mask, jax.nn.log_sigmoid(-z), 0.0)
    between = lax.cumsum(log_fail, axis=3, reverse=True) - log_fail
    a = jnp.where(mask, jnp.exp(jax.nn.log_sigmoid(z) + between), 0.0)
    return jnp.einsum('bhqk,bkhd->bqhd', a.astype(v.dtype), v)


def prompt_core(u, q, k, v, w_pool, pool_scale, sb_bias):
    B, T = u.shape[:2]
    pos = jnp.arange(T)
    pool_out = pool_mixer(u, pos, 0, w_pool, pool_scale)
    nb = T // Q_BLOCK
    qb = q.reshape(B, nb, Q_BLOCK, N_HEADS, HEAD_DIM).transpose(1, 0, 2, 3, 4)
    pb = pos.reshape(nb, Q_BLOCK)
    ob = lax.map(lambda xs: sb_attend(xs[0], k, v, xs[1], pos, sb_bias), (qb, pb))
    sb = ob.transpose(1, 0, 2, 3, 4).reshape(B, T, N_HEADS, HEAD_DIM)
    return pool_out, sb


def sample_core(u, q, k, v, k_past, v_past, pool_buf, w_pool, pool_scale, sb_bias):
    T = u.shape[1]
    past = k_past.shape[1]
    q_pos = past + jnp.arange(T)
    u_ext = jnp.concatenate([pool_buf.astype(u.dtype), u], axis=1)
    pool_out = pool_mixer(u_ext, q_pos, POOL_BUF, w_pool, pool_scale)
    k_all = jnp.concatenate([k_past.astype(k.dtype), k], axis=1)
    v_all = jnp.concatenate([v_past.astype(v.dtype), v], axis=1)
    sb = sb_attend(q, k_all, v_all, q_pos, jnp.arange(past + T), sb_bias)
    return pool_out, sb


def layer(x, mod, g_norm, w1, w3, w2, w_in, g_head, w_out, mix_core):
    sh1, sc1, gt1, sh2, sc2, gt2, sh3, sc3, gt3 = jnp.split(mod, N_MOD, axis=-1)
    x = x + 0.5 * gt1[:, None, :] * swiglu(modulate(x, g_norm[0], sh1, sc1), w1[0], w3[0], w2[0])
    B, T = x.shape[:2]
    proj = modulate(x, g_norm[1], sh2, sc2) @ w_in
    u = proj[..., :POOL_W]
    q = proj[..., POOL_W:POOL_W + ATT_W].reshape(B, T, N_HEADS, HEAD_DIM)
    k = proj[..., POOL_W + ATT_W:POOL_W + 2 * ATT_W].reshape(B, T, N_HEADS, HEAD_DIM)
    v = proj[..., POOL_W + 2 * ATT_W:].reshape(B, T, N_HEADS, HEAD_DIM)
    pool_out, sb = mix_core(u, q, k, v)
    sb = rmsnorm(sb, g_head).reshape(B, T, ATT_W)
    x = x + gt2[:, None, :] * (jnp.concatenate([pool_out, sb], axis=-1) @ w_out)
    x = x + 0.5 * gt3[:, None, :] * swiglu(modulate(x, g_norm[2], sh3, sc3), w1[1], w3[1], w2[1])
    return x, u, k, v


def setup_inputs(seed: int = 0) -> dict:
    key = jax.random.key(seed)
    ks = jax.random.split(key, 24)
    f32 = jnp.float32
    n_pages = PAST_LEN // PAGE_SIZE
    n_used = DEC_BATCH * n_pages
    n_phys = (n_used * 5) // 4
    nrm = lambda k, shape, s=1.0: jax.random.normal(k, shape, f32) * s
    page_table = jax.random.permutation(ks[0], n_phys)[:n_used].reshape(DEC_BATCH, n_pages).astype(jnp.int32)
    return {
        "x_prompt": nrm(ks[1], (BATCH, SEQ, D_MODEL)),
        "x_sample": nrm(ks[2], (DEC_BATCH, DEC_SEQ, D_MODEL)),
        "c_prompt": nrm(ks[3], (BATCH, D_MODEL)),
        "c_sample": nrm(ks[4], (DEC_BATCH, D_MODEL)),
        "cache_k": nrm(ks[5], (DEPTH, n_phys, PAGE_SIZE, N_HEADS, HEAD_DIM)),
        "cache_v": nrm(ks[6], (DEPTH, n_phys, PAGE_SIZE, N_HEADS, HEAD_DIM)),
        "state_pool": nrm(ks[7], (DEPTH, DEC_BATCH, POOL_BUF, POOL_W)),
        "page_table": page_table,
        "w_ada": nrm(ks[8], (DEPTH, D_MODEL, N_MOD * D_MODEL), 0.5 * D_MODEL ** -0.5),
        "b_ada": nrm(ks[9], (DEPTH, N_MOD * D_MODEL), 0.01),
        "g_norm": 1.0 + nrm(ks[10], (DEPTH, 3, D_MODEL), 0.1),
        "w1": nrm(ks[11], (DEPTH, 2, D_MODEL, D_FF), D_MODEL ** -0.5),
        "w3": nrm(ks[12], (DEPTH, 2, D_MODEL, D_FF), D_MODEL ** -0.5),
        "w2": nrm(ks[13], (DEPTH, 2, D_FF, D_MODEL), D_FF ** -0.5),
        "w_in": nrm(ks[14], (DEPTH, D_MODEL, POOL_W + 3 * ATT_W), D_MODEL ** -0.5),
        "w_pool": nrm(ks[15], (DEPTH, len(POOL_WINDOWS), POOL_GC, POOL_GC), POOL_GC ** -0.5),
        "pool_scale": 1.0 + nrm(ks[16], (DEPTH, POOL_W), 0.1),
        "g_head": 1.0 + nrm(ks[17], (DEPTH, N_HEADS, HEAD_DIM), 0.1),
        "sb_bias": SB_BIAS_INIT + nrm(ks[20], (DEPTH, N_HEADS), 0.1),
        "w_out": nrm(ks[18], (DEPTH, MIX_W, D_MODEL), MIX_W ** -0.5),
        "g_final": 1.0 + nrm(ks[19], (D_MODEL,), 0.1),
    }


def reference(x_prompt, x_sample, c_prompt, c_sample, cache_k, cache_v, state_pool, page_table,
              w_ada, b_ada, g_norm, w1, w3, w2, w_in, w_pool, pool_scale, g_head, sb_bias, w_out, g_final):
    xp, xs = x_prompt, x_sample
    dec_b, n_pages = page_table.shape
    kp_l, vp_l, pp_l, ks_l, vs_l, ps_l = [], [], [], [], [], []
    for l in range(DEPTH):
        mod_p = jax.nn.silu(c_prompt) @ w_ada[l] + b_ada[l]
        mod_s = jax.nn.silu(c_sample) @ w_ada[l] + b_ada[l]
        core_p = lambda u, q, k, v, l=l: prompt_core(u, q, k, v, w_pool[l], pool_scale[l], sb_bias[l])
        xp, u_p, k_p, v_p = layer(xp, mod_p, g_norm[l], w1[l], w3[l], w2[l], w_in[l], g_head[l], w_out[l], core_p)
        kp_l.append(k_p)
        vp_l.append(v_p)
        pp_l.append(u_p[:, -POOL_BUF:])
        k_past = cache_k[l][page_table].reshape(dec_b, n_pages * PAGE_SIZE, N_HEADS, HEAD_DIM)
        v_past = cache_v[l][page_table].reshape(dec_b, n_pages * PAGE_SIZE, N_HEADS, HEAD_DIM)
        buf = state_pool[l]
        core_s = lambda u, q, k, v, l=l, k_past=k_past, v_past=v_past, buf=buf: sample_core(
            u, q, k, v, k_past, v_past, buf, w_pool[l], pool_scale[l], sb_bias[l])
        xs, u_s, k_s, v_s = layer(xs, mod_s, g_norm[l], w1[l], w3[l], w2[l], w_in[l], g_head[l], w_out[l], core_s)
        ks_l.append(k_s)
        vs_l.append(v_s)
        ps_l.append(jnp.concatenate([buf.astype(u_s.dtype), u_s], axis=1)[:, -POOL_BUF:])
    y_prompt = rmsnorm(xp, g_final)
    y_sample = rmsnorm(xs, g_final)
    return (y_prompt, y_sample, jnp.stack(kp_l), jnp.stack(vp_l), jnp.stack(pp_l),
            jnp.stack(ks_l), jnp.stack(vs_l), jnp.stack(ps_l))
```

```python
import functools

import jax
import jax.numpy as jnp
from jax import lax
from jax.experimental import pallas as pl
from jax.experimental.pallas import tpu as pltpu

D_MODEL = 2048
POOL_W = 1024
ATT_W = 1024
HEAD_DIM = 128
N_HEADS = 8
POOL_WINDOWS = (2, 4, 8, 16)
POOL_GC = 256
POOL_BUF = 15
D_FF = 5632
N_MOD = 9
EPS = 1e-6
PAGE_SIZE = 128
PROJ_W = POOL_W + 3 * ATT_W
ATT_SCALE = HEAD_DIM ** -0.5

F32 = jnp.float32
BF16 = jnp.bfloat16

VMEM_LIMIT_BYTES = 56 * 1024 * 1024

FFN_TF = 256
PROJ_TN = 512
OUT_TN = 512
ADA_TN = 1024
ATT_T = 256
HALO = 16


def _params(sem):
    return pltpu.CompilerParams(dimension_semantics=sem, vmem_limit_bytes=VMEM_LIMIT_BYTES)


def _modulate(x, g, shift, scale):
    ms = jnp.mean(x * x, axis=-1, keepdims=True)
    y = x * lax.rsqrt(ms + EPS) * g
    return y * (1.0 + scale) + shift


def _rmsnorm(x, g):
    ms = jnp.mean(x * x, axis=-1, keepdims=True)
    return x * lax.rsqrt(ms + EPS) * g


def _softplus(z):
    return jnp.maximum(z, 0.0) + jnp.log1p(jnp.exp(-jnp.abs(z)))


def _split_bf16(x):
    hi = x.astype(BF16)
    lo = (x - hi.astype(F32)).astype(BF16)
    return hi, lo


def _ada_kernel(c_ref, w_ref, b_ref, o_ref):
    s = jax.nn.silu(c_ref[...]).astype(BF16)
    o_ref[...] = jnp.dot(s, w_ref[...].astype(BF16), preferred_element_type=F32) + b_ref[...]


def _ada(c_all, w_ada, b_ada):
    depth = w_ada.shape[0]
    rows = c_all.shape[0]
    per_chunk = D_MODEL // ADA_TN
    return pl.pallas_call(
        _ada_kernel,
        grid=(depth, N_MOD * per_chunk),
        in_specs=[
            pl.BlockSpec((rows, D_MODEL), lambda l, n: (0, 0)),
            pl.BlockSpec((None, D_MODEL, ADA_TN), lambda l, n: (l, 0, n)),
            pl.BlockSpec((None, 1, ADA_TN), lambda l, n: (l, 0, n)),
        ],
        out_specs=pl.BlockSpec((None, None, rows, ADA_TN),
                               lambda l, n: (l, n // per_chunk, 0, n % per_chunk)),
        out_shape=jax.ShapeDtypeStruct((depth, N_MOD, rows, D_MODEL), F32),
        compiler_params=_params(("parallel", "arbitrary")),
        name="ada",
    )(c_all, w_ada, b_ada.reshape(depth, 1, N_MOD * D_MODEL))


def _mod_specs(per_token, tm, layer, chunks, tiles_per_seq, width, col_of):
    specs = []
    for chunk in chunks:
        if per_token:
            specs.append(pl.BlockSpec((None, None, tm, width),
                                      lambda i, j, c=chunk: (layer, c, i, col_of(i, j))))
        else:
            specs.append(pl.BlockSpec((None, None, None, 1, width),
                                      lambda i, j, c=chunk: (layer, c, i // tiles_per_seq, 0, col_of(i, j))))
    return specs


def _ffn_kernel(x_ref, sh_ref, sc_ref, gt_ref, g_ref, w1_ref, w3_ref, w2_ref, gf_ref, o_ref,
                h_scr, acc_scr, *, final_norm):
    j = pl.program_id(1)

    @pl.when(j == 0)
    def _():
        h_scr[...] = _modulate(x_ref[...], g_ref[...], sh_ref[...], sc_ref[...]).astype(BF16)
        acc_scr[...] = jnp.zeros_like(acc_scr)

    h = h_scr[...]
    a = jnp.dot(h, w1_ref[...].astype(BF16), preferred_element_type=F32)
    b = jnp.dot(h, w3_ref[...].astype(BF16), preferred_element_type=F32)
    act = (jax.nn.silu(a) * b).astype(BF16)
    acc_scr[...] += jnp.dot(act, w2_ref[...].astype(BF16), preferred_element_type=F32)

    @pl.when(j == pl.num_programs(1) - 1)
    def _():
        y = x_ref[...] + 0.5 * gt_ref[...] * acc_scr[...]
        if final_norm:
            y = _rmsnorm(y, gf_ref[...])
        o_ref[...] = y


def _ffn(x, mod, g_norm4, w1, w3, w2, g_final, *, layer, which, per_token, tm, tiles_per_seq, final_norm):
    n_tok = x.shape[0]
    chunk0 = 0 if which == 0 else 6
    norm_idx = 0 if which == 0 else 2
    zero_col = lambda i, j: 0
    in_specs = [pl.BlockSpec((tm, D_MODEL), lambda i, j: (i, 0))]
    in_specs += _mod_specs(per_token, tm, layer, (chunk0, chunk0 + 1, chunk0 + 2), tiles_per_seq, D_MODEL, zero_col)
    in_specs += [
        pl.BlockSpec((None, None, 1, D_MODEL), lambda i, j: (layer, norm_idx, 0, 0)),
        pl.BlockSpec((None, None, D_MODEL, FFN_TF), lambda i, j: (layer, which, 0, j)),
        pl.BlockSpec((None, None, D_MODEL, FFN_TF), lambda i, j: (layer, which, 0, j)),
        pl.BlockSpec((None, None, FFN_TF, D_MODEL), lambda i, j: (layer, which, j, 0)),
        pl.BlockSpec((1, D_MODEL), lambda i, j: (0, 0)),
    ]
    return pl.pallas_call(
        functools.partial(_ffn_kernel, final_norm=final_norm),
        grid=(n_tok // tm, D_FF // FFN_TF),
        in_specs=in_specs,
        out_specs=pl.BlockSpec((tm, D_MODEL), lambda i, j: (i, 0)),
        out_shape=jax.ShapeDtypeStruct((n_tok, D_MODEL), F32),
        scratch_shapes=[pltpu.VMEM((tm, D_MODEL), BF16), pltpu.VMEM((tm, D_MODEL), F32)],
        compiler_params=_params(("parallel", "arbitrary")),
        name="ffn",
    )(x, mod, mod, mod, g_norm4, w1, w3, w2, g_final)


def _proj_kernel(x_ref, sh_ref, sc_ref, g_ref, w_ref, o_ref, h_scr):
    @pl.when(pl.program_id(1) == 0)
    def _():
        h_scr[...] = _modulate(x_ref[...], g_ref[...], sh_ref[...], sc_ref[...]).astype(BF16)

    o_ref[...] = jnp.dot(h_scr[...], w_ref[...].astype(BF16), preferred_element_type=F32)


def _proj(x, mod, g_norm4, w_in, *, layer, per_token, tm, tiles_per_seq):
    n_tok = x.shape[0]
    zero_col = lambda i, j: 0
    in_specs = [pl.BlockSpec((tm, D_MODEL), lambda i, j: (i, 0))]
    in_specs += _mod_specs(per_token, tm, layer, (3, 4), tiles_per_seq, D_MODEL, zero_col)
    in_specs += [
        pl.BlockSpec((None, None, 1, D_MODEL), lambda i, j: (layer, 1, 0, 0)),
        pl.BlockSpec((None, D_MODEL, PROJ_TN), lambda i, j: (layer, 0, j)),
    ]
    return pl.pallas_call(
        _proj_kernel,
        grid=(n_tok // tm, PROJ_W // PROJ_TN),
        in_specs=in_specs,
        out_specs=pl.BlockSpec((tm, PROJ_TN), lambda i, j: (i, j)),
        out_shape=jax.ShapeDtypeStruct((n_tok, PROJ_W), F32),
        scratch_shapes=[pltpu.VMEM((tm, D_MODEL), BF16)],
        compiler_params=_params(("parallel", "arbitrary")),
        name="proj",
    )(x, mod, mod, g_norm4, w_in)


def _attn_kernel(bias_ref, q_ref, k_ref, v_ref, tri_ref, gh_ref, o_ref):
    head = pl.program_id(1)
    qi = pl.program_id(2)
    q = q_ref[...].astype(BF16)
    bias = bias_ref[head]
    tri = tri_ref[...]
    row = lax.broadcasted_iota(jnp.int32, (ATT_T, ATT_T), 0)
    col = lax.broadcasted_iota(jnp.int32, (ATT_T, ATT_T), 1)
    below_diag = col < row

    def block(kb, carry, acc, diagonal):
        start = pl.multiple_of(kb * ATT_T, ATT_T)
        kblk = k_ref[pl.ds(start, ATT_T), :].astype(BF16)
        vblk = v_ref[pl.ds(start, ATT_T), :].astype(BF16)
        z = lax.dot_general(q, kblk, (((1,), (1,)), ((), ())), preferred_element_type=F32)
        z = z * ATT_SCALE + bias
        lf = -_softplus(z)
        if diagonal:
            lf = jnp.where(below_diag, lf, 0.0)
        hi, lo = _split_bf16(lf)
        within = (jnp.dot(hi, tri, preferred_element_type=F32)
                  + jnp.dot(lo, tri, preferred_element_type=F32))
        a = jnp.exp(z + lf + (within + carry))
        if diagonal:
            a = jnp.where(below_diag, a, 0.0)
        acc = acc + jnp.dot(a.astype(BF16), vblk, preferred_element_type=F32)
        carry = carry + jnp.sum(lf, axis=-1, keepdims=True)
        return carry, acc

    carry = jnp.zeros((ATT_T, 1), F32)
    acc = jnp.zeros((ATT_T, HEAD_DIM), F32)
    carry, acc = block(qi, carry, acc, True)

    def body(i, c):
        return block(qi - 1 - i, c[0], c[1], False)

    carry, acc = lax.fori_loop(0, qi, body, (carry, acc))
    o_ref[...] = _rmsnorm(acc, gh_ref[...]).astype(BF16)


def _strict_upper_ones(n):
    j = lax.broadcasted_iota(jnp.int32, (n, n), 0)
    s = lax.broadcasted_iota(jnp.int32, (n, n), 1)
    return (j > s).astype(BF16)


def _attn_prompt(proj, sb_bias_l, g_head_l, *, batch, seq):
    nq = seq // ATT_T
    q0 = POOL_W // HEAD_DIM
    k0 = (POOL_W + ATT_W) // HEAD_DIM
    v0 = (POOL_W + 2 * ATT_W) // HEAD_DIM
    return pl.pallas_call(
        _attn_kernel,
        grid=(batch, N_HEADS, nq),
        in_specs=[
            pl.BlockSpec(memory_space=pltpu.SMEM),
            pl.BlockSpec((ATT_T, HEAD_DIM), lambda b, h, i: (b * nq + i, q0 + h)),
            pl.BlockSpec((seq, HEAD_DIM), lambda b, h, i: (b, k0 + h)),
            pl.BlockSpec((seq, HEAD_DIM), lambda b, h, i: (b, v0 + h)),
            pl.BlockSpec((ATT_T, ATT_T), lambda b, h, i: (0, 0)),
            pl.BlockSpec((None, 1, HEAD_DIM), lambda b, h, i: (h, 0, 0)),
        ],
        out_specs=pl.BlockSpec((ATT_T, HEAD_DIM), lambda b, h, i: (b * nq + i, h)),
        out_shape=jax.ShapeDtypeStruct((batch * seq, ATT_W), BF16),
        compiler_params=_params(("parallel", "parallel", "arbitrary")),
        name="attn_prompt",
    )(sb_bias_l, proj, proj, proj, _strict_upper_ones(ATT_T), g_head_l.reshape(N_HEADS, 1, HEAD_DIM))


def _attn_sample_kernel(pt_ref, bias_ref, q_ref, *refs, n_pages):
    del pt_ref
    k_refs = refs[:n_pages]
    v_refs = refs[n_pages:2 * n_pages]
    tri_ref, gh_ref, o_ref = refs[2 * n_pages:]
    width = PAGE_SIZE * N_HEADS
    q = q_ref[...].astype(BF16)
    bias = bias_ref[...]
    lane = lax.broadcasted_iota(jnp.int32, (N_HEADS, width), 1)
    sub = lax.broadcasted_iota(jnp.int32, (N_HEADS, width), 0)
    own = jnp.bitwise_and(lane, N_HEADS - 1) == sub

    zs, lfs = [], []
    for p in range(n_pages):
        z = lax.dot_general(q, k_refs[p][...].astype(BF16), (((1,), (1,)), ((), ())),
                            preferred_element_type=F32)
        z = z * ATT_SCALE + bias
        zs.append(z)
        lfs.append(jnp.where(own, -_softplus(z), 0.0))

    lf_all = jnp.concatenate(lfs, axis=0)
    hi, lo = _split_bf16(lf_all)
    tri = tri_ref[...]
    within = (jnp.dot(hi, tri, preferred_element_type=F32)
              + jnp.dot(lo, tri, preferred_element_type=F32))
    totals = jnp.sum(lf_all, axis=-1, keepdims=True)

    carry = jnp.zeros((N_HEADS, 1), F32)
    acc = jnp.zeros((N_HEADS, HEAD_DIM), F32)
    for p in reversed(range(n_pages)):
        rows = slice(p * N_HEADS, (p + 1) * N_HEADS)
        a = jnp.exp(zs[p] + lfs[p] + (within[rows] + carry))
        a = jnp.where(own, a, 0.0).astype(BF16)
        acc = acc + jnp.dot(a, v_refs[p][...].astype(BF16), preferred_element_type=F32)
        carry = carry + totals[rows]
    o_ref[...] = _rmsnorm(acc, gh_ref[...])


def _attn_sample(q_s, cache_k, cache_v, page_table, sb_bias_l, g_head_l, *, layer):
    dec_b, n_pages = page_table.shape
    n_phys = cache_k.shape[1]
    width = PAGE_SIZE * N_HEADS
    ck = cache_k.reshape(cache_k.shape[0], n_phys, width, HEAD_DIM)
    cv = cache_v.reshape(cache_v.shape[0], n_phys, width, HEAD_DIM)

    def page_spec(p):
        return pl.BlockSpec((None, None, width, HEAD_DIM), lambda b, pt, p=p: (layer, pt[b, p], 0, 0))

    in_specs = [
        pl.BlockSpec((N_HEADS, 1), lambda b, pt: (0, 0)),
        pl.BlockSpec((None, N_HEADS, HEAD_DIM), lambda b, pt: (b, 0, 0)),
    ]
    in_specs += [page_spec(p) for p in range(n_pages)]
    in_specs += [page_spec(p) for p in range(n_pages)]
    in_specs += [
        pl.BlockSpec((width, width), lambda b, pt: (0, 0)),
        pl.BlockSpec((N_HEADS, HEAD_DIM), lambda b, pt: (0, 0)),
    ]
    grid_spec = pltpu.PrefetchScalarGridSpec(
        num_scalar_prefetch=1,
        grid=(dec_b,),
        in_specs=in_specs,
        out_specs=pl.BlockSpec((None, N_HEADS, HEAD_DIM), lambda b, pt: (b, 0, 0)),
    )
    return pl.pallas_call(
        functools.partial(_attn_sample_kernel, n_pages=n_pages),
        grid_spec=grid_spec,
        out_shape=jax.ShapeDtypeStruct((dec_b, N_HEADS, HEAD_DIM), F32),
        compiler_params=_params(("arbitrary",)),
        name="attn_sample",
    )(page_table, sb_bias_l.reshape(N_HEADS, 1), q_s, *([ck] * n_pages), *([cv] * n_pages),
      _strict_upper_ones(width), g_head_l)


def _pool_to_cat(diffs, wp_ref, ps_ref, sb_ref, cat_scr):
    for g in range(len(POOL_WINDOWS)):
        cols = slice(g * POOL_GC, (g + 1) * POOL_GC)
        mixed = jnp.dot(diffs[g].astype(BF16), wp_ref[g].astype(BF16), preferred_element_type=F32)
        cat_scr[:, cols] = (mixed * ps_ref[:, cols]).astype(BF16)
    cat_scr[:, POOL_W:] = sb_ref[...].astype(BF16)


def _mixout_prompt_kernel(u_ref, halo_ref, sb_ref, x_ref, gt_ref, wp_ref, ps_ref, wo_ref, o_ref,
                          ext_scr, cat_scr, *, tm, tiles_per_seq):
    i = pl.program_id(0)

    @pl.when(pl.program_id(1) == 0)
    def _():
        first = (i % tiles_per_seq) == 0
        ext_scr[0:HALO, :] = jnp.where(first, 0.0, halo_ref[...])
        ext_scr[HALO:, :] = u_ref[...]
        pos = (i % tiles_per_seq) * tm + lax.broadcasted_iota(jnp.int32, (tm, 1), 0)
        diffs = []
        for g, w in enumerate(POOL_WINDOWS):
            cols = slice(g * POOL_GC, (g + 1) * POOL_GC)
            total = ext_scr[HALO:HALO + tm, cols]
            for d in range(1, w):
                total = total + ext_scr[HALO - d:HALO - d + tm, cols]
            cnt = jnp.minimum(pos + 1, w).astype(F32)
            diffs.append(total / cnt - ext_scr[HALO:HALO + tm, cols])
        _pool_to_cat(diffs, wp_ref, ps_ref, sb_ref, cat_scr)

    mixed = jnp.dot(cat_scr[...], wo_ref[...].astype(BF16), preferred_element_type=F32)
    o_ref[...] = x_ref[...] + gt_ref[...] * mixed


def _mixout_sample_kernel(ue_ref, sb_ref, x_ref, gt_ref, wp_ref, ps_ref, wo_ref, o_ref, cat_scr):
    @pl.when(pl.program_id(1) == 0)
    def _():
        diffs = []
        for g, w in enumerate(POOL_WINDOWS):
            cols = slice(g * POOL_GC, (g + 1) * POOL_GC)
            total = ue_ref[POOL_BUF, :, cols]
            for d in range(1, w):
                total = total + ue_ref[POOL_BUF - d, :, cols]
            diffs.append(total / float(w) - ue_ref[POOL_BUF, :, cols])
        _pool_to_cat(diffs, wp_ref, ps_ref, sb_ref, cat_scr)

    mixed = jnp.dot(cat_scr[...], wo_ref[...].astype(BF16), preferred_element_type=F32)
    o_ref[...] = x_ref[...] + gt_ref[...] * mixed


def _mixout_prompt(proj, sb, x, mod, w_pool, pool_scale3, w_out, *, layer, tm, tiles_per_seq):
    n_tok = x.shape[0]
    halo_blocks = tm // HALO
    in_specs = [
        pl.BlockSpec((tm, POOL_W), lambda i, j: (i, 0)),
        pl.BlockSpec((HALO, POOL_W), lambda i, j: (jnp.maximum(i * halo_blocks - 1, 0), 0)),
        pl.BlockSpec((tm, ATT_W), lambda i, j: (i, 0)),
        pl.BlockSpec((tm, OUT_TN), lambda i, j: (i, j)),
    ]
    in_specs += _mod_specs(False, tm, layer, (5,), tiles_per_seq, OUT_TN, lambda i, j: j)
    in_specs += [
        pl.BlockSpec((None, len(POOL_WINDOWS), POOL_GC, POOL_GC), lambda i, j: (layer, 0, 0, 0)),
        pl.BlockSpec((None, 1, POOL_W), lambda i, j: (layer, 0, 0)),
        pl.BlockSpec((None, D_MODEL, OUT_TN), lambda i, j: (layer, 0, j)),
    ]
    return pl.pallas_call(
        functools.partial(_mixout_prompt_kernel, tm=tm, tiles_per_seq=tiles_per_seq),
        grid=(n_tok // tm, D_MODEL // OUT_TN),
        in_specs=in_specs,
        out_specs=pl.BlockSpec((tm, OUT_TN), lambda i, j: (i, j)),
        out_shape=jax.ShapeDtypeStruct((n_tok, D_MODEL), F32),
        scratch_shapes=[pltpu.VMEM((tm + HALO, POOL_W), F32), pltpu.VMEM((tm, D_MODEL), BF16)],
        compiler_params=_params(("parallel", "arbitrary")),
        name="mixout_prompt",
    )(proj, proj, sb, x, mod, w_pool, pool_scale3, w_out)


def _mixout_sample(u_ext, sb, x, mod, w_pool, pool_scale3, w_out, *, layer):
    n_tok = x.shape[0]
    in_specs = [
        pl.BlockSpec((POOL_BUF + 1, n_tok, POOL_W), lambda i, j: (0, 0, 0)),
        pl.BlockSpec((n_tok, ATT_W), lambda i, j: (0, 0)),
        pl.BlockSpec((n_tok, OUT_TN), lambda i, j: (0, j)),
    ]
    in_specs += _mod_specs(True, n_tok, layer, (5,), 1, OUT_TN, lambda i, j: j)
    in_specs += [
        pl.BlockSpec((None, len(POOL_WINDOWS), POOL_GC, POOL_GC), lambda i, j: (layer, 0, 0, 0)),
        pl.BlockSpec((None, 1, POOL_W), lambda i, j: (layer, 0, 0)),
        pl.BlockSpec((None, D_MODEL, OUT_TN), lambda i, j: (layer, 0, j)),
    ]
    return pl.pallas_call(
        _mixout_sample_kernel,
        grid=(1, D_MODEL // OUT_TN),
        in_specs=in_specs,
        out_specs=pl.BlockSpec((n_tok, OUT_TN), lambda i, j: (0, j)),
        out_shape=jax.ShapeDtypeStruct((n_tok, D_MODEL), F32),
        scratch_shapes=[pltpu.VMEM((n_tok, D_MODEL), BF16)],
        compiler_params=_params(("arbitrary", "arbitrary")),
        name="mixout_sample",
    )(u_ext, sb, x, mod, w_pool, pool_scale3, w_out)


def kernel(x_prompt, x_sample, c_prompt, c_sample, cache_k, cache_v, state_pool, page_table,
           w_ada, b_ada, g_norm, w1, w3, w2, w_in, w_pool, pool_scale, g_head, sb_bias, w_out, g_final):
    batch, seq, _ = x_prompt.shape
    dec_b = x_sample.shape[0]
    depth = w_ada.shape[0]
    tm_p = 512
    tiles_per_seq = seq // tm_p

    pad = (-(dec_b + batch)) % 16
    c_all = jnp.concatenate([c_sample, c_prompt, jnp.zeros((pad, D_MODEL), F32)], axis=0)
    mod = _ada(c_all, w_ada, b_ada)
    mod_s = mod[:, :, :dec_b]
    mod_p = mod[:, :, dec_b:dec_b + batch].reshape(depth, N_MOD, batch, 1, D_MODEL)

    g_norm4 = g_norm.reshape(depth, 3, 1, D_MODEL)
    pool_scale3 = pool_scale.reshape(depth, 1, POOL_W)
    g_final2 = g_final.reshape(1, D_MODEL)

    xp = x_prompt.reshape(batch * seq, D_MODEL)
    xs = x_sample.reshape(dec_b, D_MODEL)
    kp_l, vp_l, pp_l, ks_l, vs_l, ps_l = [], [], [], [], [], []
    for l in range(depth):
        last = l == depth - 1
        ffn_p = functools.partial(_ffn, mod=mod_p, g_norm4=g_norm4, w1=w1, w3=w3, w2=w2, g_final=g_final2,
                                  layer=l, per_token=False, tm=tm_p, tiles_per_seq=tiles_per_seq)
        xp = ffn_p(xp, which=0, final_norm=False)
        proj_p = _proj(xp, mod_p, g_norm4, w_in, layer=l, per_token=False, tm=tm_p, tiles_per_seq=tiles_per_seq)
        sb_p = _attn_prompt(proj_p, sb_bias[l], g_head[l], batch=batch, seq=seq)
        xp = _mixout_prompt(proj_p, sb_p, xp, mod_p, w_pool, pool_scale3, w_out,
                            layer=l, tm=tm_p, tiles_per_seq=tiles_per_seq)
        xp = ffn_p(xp, which=1, final_norm=last)
        proj_p4 = proj_p.reshape(batch, seq, PROJ_W)
        kp_l.append(proj_p4[..., POOL_W + ATT_W:POOL_W + 2 * ATT_W].reshape(batch, seq, N_HEADS, HEAD_DIM))
        vp_l.append(proj_p4[..., POOL_W + 2 * ATT_W:].reshape(batch, seq, N_HEADS, HEAD_DIM))
        pp_l.append(proj_p4[:, seq - POOL_BUF:, :POOL_W])

        ffn_s = functools.partial(_ffn, mod=mod_s, g_norm4=g_norm4, w1=w1, w3=w3, w2=w2, g_final=g_final2,
                                  layer=l, per_token=True, tm=dec_b, tiles_per_seq=1)
        xs = ffn_s(xs, which=0, final_norm=False)
        proj_s = _proj(xs, mod_s, g_norm4, w_in, layer=l, per_token=True, tm=dec_b, tiles_per_seq=1)
        u_s = proj_s[:, :POOL_W]
        q_s = proj_s[:, POOL_W:POOL_W + ATT_W].reshape(dec_b, N_HEADS, HEAD_DIM)
        sb_s = _attn_sample(q_s, cache_k, cache_v, page_table, sb_bias[l], g_head[l], layer=l)
        u_ext = jnp.concatenate([jnp.transpose(state_pool[l], (1, 0, 2)), u_s[None]], axis=0)
        xs = _mixout_sample(u_ext, sb_s.reshape(dec_b, ATT_W), xs, mod_s, w_pool, pool_scale3, w_out, layer=l)
        xs = ffn_s(xs, which=1, final_norm=last)
        ks_l.append(proj_s[:, POOL_W + ATT_W:POOL_W + 2 * ATT_W].reshape(dec_b, 1, N_HEADS, HEAD_DIM))
        vs_l.append(proj_s[:, POOL_W + 2 * ATT_W:].reshape(dec_b, 1, N_HEADS, HEAD_DIM))
        ps_l.append(jnp.concatenate([state_pool[l][:, 1:], u_s[:, None, :]], axis=1))

    y_prompt = xp.reshape(batch, seq, D_MODEL)
    y_sample = xs.reshape(dec_b, 1, D_MODEL)
    return (y_prompt, y_sample, jnp.stack(kp_l), jnp.stack(vp_l), jnp.stack(pp_l),
            jnp.stack(ks_l), jnp.stack(vs_l), jnp.stack(ps_l))
```

```python
import functools

import jax
import jax.numpy as jnp
from jax import lax
from jax.experimental import pallas as pl
from jax.experimental.pallas import tpu as pltpu

D_MODEL = 2048
POOL_W = 1024
ATT_W = 1024
HEAD_DIM = 128
N_HEADS = 8
POOL_WINDOWS = (2, 4, 8, 16)
POOL_GC = 256
POOL_BUF = 15
D_FF = 5632
N_MOD = 9
EPS = 1e-6
PAGE_SIZE = 128
PROJ_W = POOL_W + 3 * ATT_W
ATT_SCALE = HEAD_DIM ** -0.5

F32 = jnp.float32
BF16 = jnp.bfloat16

VMEM_LIMIT_BYTES = 56 * 1024 * 1024

TM_PROMPT = 1024
FFN_TF = 256
PROJ_TN = 512
OUT_TN = 512
ADA_TN = 1024
ATT_T = 256
ATT_G = 4
HALO = 16


def _params(sem):
    return pltpu.CompilerParams(dimension_semantics=sem, vmem_limit_bytes=VMEM_LIMIT_BYTES)


def _modulate(x, g, shift, scale):
    ms = jnp.mean(x * x, axis=-1, keepdims=True)
    y = x * lax.rsqrt(ms + EPS) * g
    return y * (1.0 + scale) + shift


def _rmsnorm(x, g):
    ms = jnp.mean(x * x, axis=-1, keepdims=True)
    return x * lax.rsqrt(ms + EPS) * g


def _softplus(z):
    return jnp.maximum(z, 0.0) + jnp.log(1.0 + jnp.exp(-jnp.abs(z)))


def _split_bf16(x):
    hi = x.astype(BF16)
    lo = (x - hi.astype(F32)).astype(BF16)
    return hi, lo


def _ada_kernel(c_ref, w_ref, b_ref, o_ref):
    s = jax.nn.silu(c_ref[...]).astype(BF16)
    o_ref[...] = jnp.dot(s, w_ref[...].astype(BF16), preferred_element_type=F32) + b_ref[...]


def _ada(c_all, w_ada, b_ada):
    depth = w_ada.shape[0]
    rows = c_all.shape[0]
    per_chunk = D_MODEL // ADA_TN
    return pl.pallas_call(
        _ada_kernel,
        grid=(depth, N_MOD * per_chunk),
        in_specs=[
            pl.BlockSpec((rows, D_MODEL), lambda l, n: (0, 0)),
            pl.BlockSpec((None, D_MODEL, ADA_TN), lambda l, n: (l, 0, n)),
            pl.BlockSpec((None, 1, ADA_TN), lambda l, n: (l, 0, n)),
        ],
        out_specs=pl.BlockSpec((None, None, rows, ADA_TN),
                               lambda l, n: (l, n // per_chunk, 0, n % per_chunk)),
        out_shape=jax.ShapeDtypeStruct((depth, N_MOD, rows, D_MODEL), F32),
        compiler_params=_params(("parallel", "arbitrary")),
        name="ada",
    )(c_all, w_ada, b_ada.reshape(depth, 1, N_MOD * D_MODEL))


def _mod_specs(per_token, tm, layer, chunks, tiles_per_seq, width, col_of):
    specs = []
    for chunk in chunks:
        if per_token:
            specs.append(pl.BlockSpec((None, None, tm, width),
                                      lambda i, j, c=chunk: (layer, c, i, col_of(i, j))))
        else:
            specs.append(pl.BlockSpec((None, None, None, 1, width),
                                      lambda i, j, c=chunk: (layer, c, i // tiles_per_seq, 0, col_of(i, j))))
    return specs


def _ffn_kernel(x_ref, sh_ref, sc_ref, gt_ref, g_ref, w1_ref, w3_ref, w2_ref, gf_ref, o_ref,
                h_scr, *, final_norm):
    j = pl.program_id(1)

    @pl.when(j == 0)
    def _():
        h_scr[...] = _modulate(x_ref[...], g_ref[...], sh_ref[...], sc_ref[...]).astype(BF16)
        o_ref[...] = jnp.zeros_like(o_ref)

    h = h_scr[...]
    a = jnp.dot(h, w1_ref[...].astype(BF16), preferred_element_type=F32)
    b = jnp.dot(h, w3_ref[...].astype(BF16), preferred_element_type=F32)
    act = (jax.nn.silu(a) * b).astype(BF16)
    o_ref[...] += jnp.dot(act, w2_ref[...].astype(BF16), preferred_element_type=F32)

    @pl.when(j == pl.num_programs(1) - 1)
    def _():
        y = x_ref[...] + 0.5 * gt_ref[...] * o_ref[...]
        if final_norm:
            y = _rmsnorm(y, gf_ref[...])
        o_ref[...] = y


def _ffn(x, mod, g_norm4, w1, w3, w2, g_final, *, layer, which, per_token, tm, tiles_per_seq, final_norm):
    n_tok = x.shape[0]
    chunk0 = 0 if which == 0 else 6
    norm_idx = 0 if which == 0 else 2
    zero_col = lambda i, j: 0
    in_specs = [pl.BlockSpec((tm, D_MODEL), lambda i, j: (i, 0), pipeline_mode=pl.Buffered(1))]
    in_specs += _mod_specs(per_token, tm, layer, (chunk0, chunk0 + 1, chunk0 + 2), tiles_per_seq, D_MODEL, zero_col)
    in_specs += [
        pl.BlockSpec((None, None, 1, D_MODEL), lambda i, j: (layer, norm_idx, 0, 0)),
        pl.BlockSpec((None, None, D_MODEL, FFN_TF), lambda i, j: (layer, which, 0, j)),
        pl.BlockSpec((None, None, D_MODEL, FFN_TF), lambda i, j: (layer, which, 0, j)),
        pl.BlockSpec((None, None, FFN_TF, D_MODEL), lambda i, j: (layer, which, j, 0)),
        pl.BlockSpec((1, D_MODEL), lambda i, j: (0, 0)),
    ]
    return pl.pallas_call(
        functools.partial(_ffn_kernel, final_norm=final_norm),
        grid=(n_tok // tm, D_FF // FFN_TF),
        in_specs=in_specs,
        out_specs=pl.BlockSpec((tm, D_MODEL), lambda i, j: (i, 0)),
        out_shape=jax.ShapeDtypeStruct((n_tok, D_MODEL), F32),
        scratch_shapes=[pltpu.VMEM((tm, D_MODEL), BF16)],
        compiler_params=_params(("parallel", "arbitrary")),
        name="ffn",
    )(x, mod, mod, mod, g_norm4, w1, w3, w2, g_final)


_PROJ_HALVES = POOL_W // PROJ_TN


def _proj_kernel(*refs):
    x_ref, sh_ref, sc_ref, g_ref, w_ref = refs[:5]
    u_ref, q_ref, k_ref, v_ref, h_scr = refs[-5:]
    j = pl.program_id(1)

    @pl.when(j == 0)
    def _():
        h_scr[...] = _modulate(x_ref[...], g_ref[...], sh_ref[...], sc_ref[...]).astype(BF16)

    res = jnp.dot(h_scr[...], w_ref[...].astype(BF16), preferred_element_type=F32)
    for idx, ref in enumerate((u_ref, q_ref, k_ref, v_ref)):
        @pl.when(j // _PROJ_HALVES == idx)
        def _(ref=ref):
            ref[...] = res.astype(ref.dtype)


def _proj(x, mod, g_norm4, w_in, kv_prev, *, layer, depth, per_token, tm, tiles_per_seq):
    n_tok = x.shape[0]
    zero_col = lambda i, j: 0

    def half_of(idx):
        return lambda i, j: jnp.clip(j - idx * _PROJ_HALVES, 0, _PROJ_HALVES - 1)

    in_specs = [pl.BlockSpec((tm, D_MODEL), lambda i, j: (i, 0))]
    in_specs += _mod_specs(per_token, tm, layer, (3, 4), tiles_per_seq, D_MODEL, zero_col)
    in_specs += [
        pl.BlockSpec((None, None, 1, D_MODEL), lambda i, j: (layer, 1, 0, 0)),
        pl.BlockSpec((None, D_MODEL, PROJ_TN), lambda i, j: (layer, 0, j)),
    ]
    operands = [x, mod, mod, g_norm4, w_in]
    aliases = {}
    if kv_prev is not None:
        in_specs += [pl.BlockSpec(memory_space=pl.ANY), pl.BlockSpec(memory_space=pl.ANY)]
        aliases = {len(operands): 2, len(operands) + 1: 3}
        operands += list(kv_prev)
    out_specs = [
        pl.BlockSpec((tm, PROJ_TN), lambda i, j: (i, half_of(0)(i, j))),
        pl.BlockSpec((tm, PROJ_TN), lambda i, j: (i, half_of(1)(i, j))),
        pl.BlockSpec((None, tm, PROJ_TN), lambda i, j: (layer, i, half_of(2)(i, j))),
        pl.BlockSpec((None, tm, PROJ_TN), lambda i, j: (layer, i, half_of(3)(i, j))),
    ]
    out_shape = [
        jax.ShapeDtypeStruct((n_tok, POOL_W), F32),
        jax.ShapeDtypeStruct((n_tok, ATT_W), BF16),
        jax.ShapeDtypeStruct((depth, n_tok, ATT_W), F32),
        jax.ShapeDtypeStruct((depth, n_tok, ATT_W), F32),
    ]
    return pl.pallas_call(
        _proj_kernel,
        grid=(n_tok // tm, PROJ_W // PROJ_TN),
        in_specs=in_specs,
        out_specs=out_specs,
        out_shape=out_shape,
        input_output_aliases=aliases,
        scratch_shapes=[pltpu.VMEM((tm, D_MODEL), BF16)],
        compiler_params=_params(("parallel", "arbitrary")),
        name="proj",
    )(*operands)


def _attn_kernel(bias_ref, q_ref, k_ref, v_ref, tri_ref, gh_ref, o_ref):
    group = pl.program_id(1)
    qi = pl.program_id(2)
    tri = tri_ref[...]
    row = lax.broadcasted_iota(jnp.int32, (ATT_T, ATT_T), 0)
    col = lax.broadcasted_iota(jnp.int32, (ATT_T, ATT_T), 1)
    below_diag = col < row
    heads = [slice(g * HEAD_DIM, (g + 1) * HEAD_DIM) for g in range(ATT_G)]
    qs = [q_ref[:, cols] for cols in heads]
    biases = [bias_ref[group * ATT_G + g] for g in range(ATT_G)]

    def block(kb, state, diagonal):
        start = pl.multiple_of(kb * ATT_T, ATT_T)
        out = []
        for g, cols in enumerate(heads):
            carry, acc = state[g]
            kblk = k_ref[pl.ds(start, ATT_T), cols].astype(BF16)
            vblk = v_ref[pl.ds(start, ATT_T), cols].astype(BF16)
            z = lax.dot_general(qs[g], kblk, (((1,), (1,)), ((), ())), preferred_element_type=F32)
            z = z * ATT_SCALE + biases[g]
            lf = -_softplus(z)
            if diagonal:
                lf = jnp.where(below_diag, lf, 0.0)
            log_hit = z + lf
            total = jnp.sum(lf, axis=-1, keepdims=True)
            hi, lo = _split_bf16(lf)
            within = (jnp.dot(hi, tri, preferred_element_type=F32)
                      + jnp.dot(lo, tri, preferred_element_type=F32))
            a = jnp.exp(log_hit + (within + carry))
            if diagonal:
                a = jnp.where(below_diag, a, 0.0)
            acc = acc + jnp.dot(a.astype(BF16), vblk, preferred_element_type=F32)
            out.append((carry + total, acc))
        return tuple(out)

    state = tuple((jnp.zeros((ATT_T, 1), F32), jnp.zeros((ATT_T, HEAD_DIM), F32)) for _ in heads)
    state = block(qi, state, True)
    state = lax.fori_loop(0, qi, lambda i, s: block(qi - 1 - i, s, False), state)
    for g, cols in enumerate(heads):
        o_ref[:, cols] = _rmsnorm(state[g][1], gh_ref[g:g + 1, :]).astype(BF16)


def _strict_upper_ones(n):
    j = lax.broadcasted_iota(jnp.int32, (n, n), 0)
    s = lax.broadcasted_iota(jnp.int32, (n, n), 1)
    return (j > s).astype(BF16)


def _attn_prompt(q, k_all, v_all, sb_bias_l, g_head_l, *, layer, batch, seq):
    nq = seq // ATT_T
    gw = ATT_G * HEAD_DIM
    return pl.pallas_call(
        _attn_kernel,
        grid=(batch, N_HEADS // ATT_G, nq),
        in_specs=[
            pl.BlockSpec(memory_space=pltpu.SMEM),
            pl.BlockSpec((ATT_T, gw), lambda b, h, i: (b * nq + i, h)),
            pl.BlockSpec((None, seq, gw), lambda b, h, i: (layer, b, h)),
            pl.BlockSpec((None, seq, gw), lambda b, h, i: (layer, b, h)),
            pl.BlockSpec((ATT_T, ATT_T), lambda b, h, i: (0, 0)),
            pl.BlockSpec((None, ATT_G, HEAD_DIM), lambda b, h, i: (h, 0, 0)),
        ],
        out_specs=pl.BlockSpec((ATT_T, gw), lambda b, h, i: (b * nq + i, h)),
        out_shape=jax.ShapeDtypeStruct((batch * seq, ATT_W), BF16),
        compiler_params=_params(("parallel", "parallel", "arbitrary")),
        name="attn_prompt",
    )(sb_bias_l, q, k_all, v_all, _strict_upper_ones(ATT_T),
      g_head_l.reshape(N_HEADS // ATT_G, ATT_G, HEAD_DIM))


def _attn_sample_kernel(pt_ref, bias_ref, q_ref, *refs, n_pages):
    del pt_ref
    k_refs = refs[:n_pages]
    v_refs = refs[n_pages:2 * n_pages]
    tri_ref, gh_ref, o_ref = refs[2 * n_pages:]
    width = PAGE_SIZE * N_HEADS
    q = q_ref[...].astype(BF16)
    bias = bias_ref[...]
    lane = lax.broadcasted_iota(jnp.int32, (N_HEADS, width), 1)
    sub = lax.broadcasted_iota(jnp.int32, (N_HEADS, width), 0)
    own = jnp.bitwise_and(lane, N_HEADS - 1) == sub

    zs, lfs = [], []
    for p in range(n_pages):
        z = lax.dot_general(q, k_refs[p][...].astype(BF16), (((1,), (1,)), ((), ())),
                            preferred_element_type=F32)
        z = z * ATT_SCALE + bias
        zs.append(z)
        lfs.append(jnp.where(own, -_softplus(z), 0.0))

    lf_all = jnp.concatenate(lfs, axis=0)
    hi, lo = _split_bf16(lf_all)
    tri = tri_ref[...]
    within = (jnp.dot(hi, tri, preferred_element_type=F32)
              + jnp.dot(lo, tri, preferred_element_type=F32))
    totals = jnp.sum(lf_all, axis=-1, keepdims=True)

    carry = jnp.zeros((N_HEADS, 1), F32)
    acc = jnp.zeros((N_HEADS, HEAD_DIM), F32)
    for p in reversed(range(n_pages)):
        rows = slice(p * N_HEADS, (p + 1) * N_HEADS)
        a = jnp.exp(zs[p] + lfs[p] + (within[rows] + carry))
        a = jnp.where(own, a, 0.0).astype(BF16)
        acc = acc + jnp.dot(a, v_refs[p][...].astype(BF16), preferred_element_type=F32)
        carry = carry + totals[rows]
    o_ref[...] = _rmsnorm(acc, gh_ref[...])


def _attn_sample(q_s, cache_k, cache_v, page_table, sb_bias_l, g_head_l, *, layer):
    dec_b, n_pages = page_table.shape
    n_phys = cache_k.shape[1]
    width = PAGE_SIZE * N_HEADS
    ck = cache_k.reshape(cache_k.shape[0], n_phys, width, HEAD_DIM)
    cv = cache_v.reshape(cache_v.shape[0], n_phys, width, HEAD_DIM)

    def page_spec(p):
        return pl.BlockSpec((None, None, width, HEAD_DIM), lambda b, pt, p=p: (layer, pt[b, p], 0, 0))

    in_specs = [
        pl.BlockSpec((N_HEADS, 1), lambda b, pt: (0, 0)),
        pl.BlockSpec((None, N_HEADS, HEAD_DIM), lambda b, pt: (b, 0, 0)),
    ]
    in_specs += [page_spec(p) for p in range(n_pages)]
    in_specs += [page_spec(p) for p in range(n_pages)]
    in_specs += [
        pl.BlockSpec((width, width), lambda b, pt: (0, 0)),
        pl.BlockSpec((N_HEADS, HEAD_DIM), lambda b, pt: (0, 0)),
    ]
    grid_spec = pltpu.PrefetchScalarGridSpec(
        num_scalar_prefetch=1,
        grid=(dec_b,),
        in_specs=in_specs,
        out_specs=pl.BlockSpec((None, N_HEADS, HEAD_DIM), lambda b, pt: (b, 0, 0)),
    )
    return pl.pallas_call(
        functools.partial(_attn_sample_kernel, n_pages=n_pages),
        grid_spec=grid_spec,
        out_shape=jax.ShapeDtypeStruct((dec_b, N_HEADS, HEAD_DIM), F32),
        compiler_params=_params(("arbitrary",)),
        name="attn_sample",
    )(page_table, sb_bias_l.reshape(N_HEADS, 1), q_s, *([ck] * n_pages), *([cv] * n_pages),
      _strict_upper_ones(width), g_head_l)


def _pool_to_cat(diffs, wp_ref, ps_ref, sb_ref, cat_scr):
    for g in range(len(POOL_WINDOWS)):
        cols = slice(g * POOL_GC, (g + 1) * POOL_GC)
        mixed = jnp.dot(diffs[g].astype(BF16), wp_ref[g].astype(BF16), preferred_element_type=F32)
        cat_scr[:, cols] = (mixed * ps_ref[:, cols]).astype(BF16)
    cat_scr[:, POOL_W:] = sb_ref[...].astype(BF16)


def _mixout_prompt_kernel(u_ref, halo_ref, sb_ref, x_ref, gt_ref, wp_ref, ps_ref, wo_ref, o_ref,
                          ext_scr, cat_scr, *, tm, tiles_per_seq):
    i = pl.program_id(0)

    @pl.when(pl.program_id(1) == 0)
    def _():
        first = (i % tiles_per_seq) == 0
        ext_scr[0:HALO, :] = jnp.where(first, 0.0, halo_ref[...])
        ext_scr[HALO:, :] = u_ref[...]
        pos = (i % tiles_per_seq) * tm + lax.broadcasted_iota(jnp.int32, (tm, 1), 0)
        diffs = []
        for g, w in enumerate(POOL_WINDOWS):
            cols = slice(g * POOL_GC, (g + 1) * POOL_GC)
            total = ext_scr[HALO:HALO + tm, cols]
            for d in range(1, w):
                total = total + ext_scr[HALO - d:HALO - d + tm, cols]
            cnt = jnp.minimum(pos + 1, w).astype(F32)
            diffs.append(total / cnt - ext_scr[HALO:HALO + tm, cols])
        _pool_to_cat(diffs, wp_ref, ps_ref, sb_ref, cat_scr)

    mixed = jnp.dot(cat_scr[...], wo_ref[...].astype(BF16), preferred_element_type=F32)
    o_ref[...] = x_ref[...] + gt_ref[...] * mixed


def _mixout_sample_kernel(ue_ref, sb_ref, x_ref, gt_ref, wp_ref, ps_ref, wo_ref, o_ref, cat_scr):
    @pl.when(pl.program_id(1) == 0)
    def _():
        diffs = []
        for g, w in enumerate(POOL_WINDOWS):
            cols = slice(g * POOL_GC, (g + 1) * POOL_GC)
            total = ue_ref[POOL_BUF, :, cols]
            for d in range(1, w):
                total = total + ue_ref[POOL_BUF - d, :, cols]
            diffs.append(total / float(w) - ue_ref[POOL_BUF, :, cols])
        _pool_to_cat(diffs, wp_ref, ps_ref, sb_ref, cat_scr)

    mixed = jnp.dot(cat_scr[...], wo_ref[...].astype(BF16), preferred_element_type=F32)
    o_ref[...] = x_ref[...] + gt_ref[...] * mixed


def _mixout_prompt(u, sb, x, mod, w_pool, pool_scale3, w_out, *, layer, tm, tiles_per_seq):
    n_tok = x.shape[0]
    halo_blocks = tm // HALO
    in_specs = [
        pl.BlockSpec((tm, POOL_W), lambda i, j: (i, 0)),
        pl.BlockSpec((HALO, POOL_W), lambda i, j: (jnp.maximum(i * halo_blocks - 1, 0), 0)),
        pl.BlockSpec((tm, ATT_W), lambda i, j: (i, 0)),
        pl.BlockSpec((tm, OUT_TN), lambda i, j: (i, j)),
    ]
    in_specs += _mod_specs(False, tm, layer, (5,), tiles_per_seq, OUT_TN, lambda i, j: j)
    in_specs += [
        pl.BlockSpec((None, len(POOL_WINDOWS), POOL_GC, POOL_GC), lambda i, j: (layer, 0, 0, 0)),
        pl.BlockSpec((None, 1, POOL_W), lambda i, j: (layer, 0, 0)),
        pl.BlockSpec((None, D_MODEL, OUT_TN), lambda i, j: (layer, 0, j)),
    ]
    return pl.pallas_call(
        functools.partial(_mixout_prompt_kernel, tm=tm, tiles_per_seq=tiles_per_seq),
        grid=(n_tok // tm, D_MODEL // OUT_TN),
        in_specs=in_specs,
        out_specs=pl.BlockSpec((tm, OUT_TN), lambda i, j: (i, j)),
        out_shape=jax.ShapeDtypeStruct((n_tok, D_MODEL), F32),
        scratch_shapes=[pltpu.VMEM((tm + HALO, POOL_W), F32), pltpu.VMEM((tm, D_MODEL), BF16)],
        compiler_params=_params(("parallel", "arbitrary")),
        name="mixout_prompt",
    )(u, u, sb, x, mod, w_pool, pool_scale3, w_out)


def _mixout_sample(u_ext, sb, x, mod, w_pool, pool_scale3, w_out, *, layer):
    n_tok = x.shape[0]
    in_specs = [
        pl.BlockSpec((POOL_BUF + 1, n_tok, POOL_W), lambda i, j: (0, 0, 0)),
        pl.BlockSpec((n_tok, ATT_W), lambda i, j: (0, 0)),
        pl.BlockSpec((n_tok, OUT_TN), lambda i, j: (0, j)),
    ]
    in_specs += _mod_specs(True, n_tok, layer, (5,), 1, OUT_TN, lambda i, j: j)
    in_specs += [
        pl.BlockSpec((None, len(POOL_WINDOWS), POOL_GC, POOL_GC), lambda i, j: (layer, 0, 0, 0)),
        pl.BlockSpec((None, 1, POOL_W), lambda i, j: (layer, 0, 0)),
        pl.BlockSpec((None, D_MODEL, OUT_TN), lambda i, j: (layer, 0, j)),
    ]
    return pl.pallas_call(
        _mixout_sample_kernel,
        grid=(1, D_MODEL // OUT_TN),
        in_specs=in_specs,
        out_specs=pl.BlockSpec((n_tok, OUT_TN), lambda i, j: (0, j)),
        out_shape=jax.ShapeDtypeStruct((n_tok, D_MODEL), F32),
        scratch_shapes=[pltpu.VMEM((n_tok, D_MODEL), BF16)],
        compiler_params=_params(("arbitrary", "arbitrary")),
        name="mixout_sample",
    )(u_ext, sb, x, mod, w_pool, pool_scale3, w_out)


def kernel(x_prompt, x_sample, c_prompt, c_sample, cache_k, cache_v, state_pool, page_table,
           w_ada, b_ada, g_norm, w1, w3, w2, w_in, w_pool, pool_scale, g_head, sb_bias, w_out, g_final):
    batch, seq, _ = x_prompt.shape
    dec_b = x_sample.shape[0]
    depth = w_ada.shape[0]
    tm_p = TM_PROMPT
    tiles_per_seq = seq // tm_p

    pad = (-(dec_b + batch)) % 16
    c_all = jnp.concatenate([c_sample, c_prompt, jnp.zeros((pad, D_MODEL), F32)], axis=0)
    mod = _ada(c_all, w_ada, b_ada)
    mod_s = mod[:, :, :dec_b]
    mod_p = mod[:, :, dec_b:dec_b + batch].reshape(depth, N_MOD, batch, 1, D_MODEL)

    g_norm4 = g_norm.reshape(depth, 3, 1, D_MODEL)
    pool_scale3 = pool_scale.reshape(depth, 1, POOL_W)
    g_final2 = g_final.reshape(1, D_MODEL)

    xp = x_prompt.reshape(batch * seq, D_MODEL)
    xs = x_sample.reshape(dec_b, D_MODEL)
    kv_p = kv_s = None
    pp_l, ps_l = [], []
    for l in range(depth):
        last = l == depth - 1
        ffn_p = functools.partial(_ffn, mod=mod_p, g_norm4=g_norm4, w1=w1, w3=w3, w2=w2, g_final=g_final2,
                                  layer=l, per_token=False, tm=tm_p, tiles_per_seq=tiles_per_seq)
        xp = ffn_p(xp, which=0, final_norm=False)
        u_p, q_p, k_p, v_p = _proj(xp, mod_p, g_norm4, w_in, kv_p, layer=l, depth=depth,
                                   per_token=False, tm=tm_p, tiles_per_seq=tiles_per_seq)
        kv_p = (k_p, v_p)
        sb_p = _attn_prompt(q_p, k_p, v_p, sb_bias[l], g_head[l], layer=l, batch=batch, seq=seq)
        xp = _mixout_prompt(u_p, sb_p, xp, mod_p, w_pool, pool_scale3, w_out,
                            layer=l, tm=tm_p, tiles_per_seq=tiles_per_seq)
        xp = ffn_p(xp, which=1, final_norm=last)
        pp_l.append(u_p.reshape(batch, seq, POOL_W)[:, seq - POOL_BUF:])

        ffn_s = functools.partial(_ffn, mod=mod_s, g_norm4=g_norm4, w1=w1, w3=w3, w2=w2, g_final=g_final2,
                                  layer=l, per_token=True, tm=dec_b, tiles_per_seq=1)
        xs = ffn_s(xs, which=0, final_norm=False)
        u_s, q_s, k_s, v_s = _proj(xs, mod_s, g_norm4, w_in, kv_s, layer=l, depth=depth,
                                   per_token=True, tm=dec_b, tiles_per_seq=1)
        kv_s = (k_s, v_s)
        q_s3 = q_s.astype(F32).reshape(dec_b, N_HEADS, HEAD_DIM)
        sb_s = _attn_sample(q_s3, cache_k, cache_v, page_table, sb_bias[l], g_head[l], layer=l)
        u_ext = jnp.concatenate([jnp.transpose(state_pool[l], (1, 0, 2)), u_s[None]], axis=0)
        xs = _mixout_sample(u_ext, sb_s.reshape(dec_b, ATT_W), xs, mod_s, w_pool, pool_scale3, w_out, layer=l)
        xs = ffn_s(xs, which=1, final_norm=last)
        ps_l.append(jnp.concatenate([state_pool[l][:, 1:], u_s[:, None, :]], axis=1))

    y_prompt = xp.reshape(batch, seq, D_MODEL)
    y_sample = xs.reshape(dec_b, 1, D_MODEL)
    kv5 = lambda a, rows, t: a.reshape(depth, rows, t, N_HEADS, HEAD_DIM)
    return (y_prompt, y_sample, kv5(kv_p[0], batch, seq), kv5(kv_p[1], batch, seq), jnp.stack(pp_l),
            kv5(kv_s[0], dec_b, 1), kv5(kv_s[1], dec_b, 1), jnp.stack(ps_l))
```

```python
import functools

import jax
import jax.numpy as jnp
from jax import lax
from jax.experimental import pallas as pl
from jax.experimental.pallas import tpu as pltpu

D_MODEL = 2048
POOL_W = 1024
ATT_W = 1024
HEAD_DIM = 128
N_HEADS = 8
POOL_WINDOWS = (2, 4, 8, 16)
POOL_GC = 256
POOL_BUF = 15
D_FF = 5632
N_MOD = 9
EPS = 1e-6
PAGE_SIZE = 128
PROJ_W = POOL_W + 3 * ATT_W
ATT_SCALE = HEAD_DIM ** -0.5

F32 = jnp.float32
BF16 = jnp.bfloat16

VMEM_LIMIT_BYTES = 56 * 1024 * 1024

TM_PROMPT = 1024
TM_MIX = 512
FFN_TF = 256
PROJ_TN = 512
OUT_TN = 512
ADA_TN = 1024
ATT_T = 256
ATT_G = 8
ROW_CHUNK = 128
HALO = 16


def _params(sem):
    return pltpu.CompilerParams(dimension_semantics=sem, vmem_limit_bytes=VMEM_LIMIT_BYTES)


def _modulate(x, g, shift, scale):
    ms = jnp.mean(x * x, axis=-1, keepdims=True)
    y = x * lax.rsqrt(ms + EPS) * g
    return y * (1.0 + scale) + shift


def _rmsnorm(x, g):
    ms = jnp.mean(x * x, axis=-1, keepdims=True)
    return x * lax.rsqrt(ms + EPS) * g


def _softplus(z):
    return jnp.maximum(z, 0.0) + jnp.log(1.0 + jnp.exp(-jnp.abs(z)))


def _split_bf16(x):
    hi = x.astype(BF16)
    lo = (x - hi.astype(F32)).astype(BF16)
    return hi, lo


def _ada_kernel(c_ref, w_ref, b_ref, o_ref):
    s = jax.nn.silu(c_ref[...]).astype(BF16)
    o_ref[...] = jnp.dot(s, w_ref[...].astype(BF16), preferred_element_type=F32) + b_ref[...]


def _ada(c_all, w_ada, b_ada):
    depth = w_ada.shape[0]
    rows = c_all.shape[0]
    per_chunk = D_MODEL // ADA_TN
    return pl.pallas_call(
        _ada_kernel,
        grid=(depth, N_MOD * per_chunk),
        in_specs=[
            pl.BlockSpec((rows, D_MODEL), lambda l, n: (0, 0)),
            pl.BlockSpec((None, D_MODEL, ADA_TN), lambda l, n: (l, 0, n)),
            pl.BlockSpec((None, 1, ADA_TN), lambda l, n: (l, 0, n)),
        ],
        out_specs=pl.BlockSpec((None, None, rows, ADA_TN),
                               lambda l, n: (l, n // per_chunk, 0, n % per_chunk)),
        out_shape=jax.ShapeDtypeStruct((depth, N_MOD, rows, D_MODEL), F32),
        compiler_params=_params(("parallel", "arbitrary")),
        name="ada",
    )(c_all, w_ada, b_ada.reshape(depth, 1, N_MOD * D_MODEL))


def _mod_specs(per_token, tm, layer, chunks, tiles_per_seq, width, col_of):
    specs = []
    for chunk in chunks:
        if per_token:
            specs.append(pl.BlockSpec((None, None, tm, width),
                                      lambda i, j, c=chunk: (layer, c, i, col_of(i, j))))
        else:
            specs.append(pl.BlockSpec((None, None, None, 1, width),
                                      lambda i, j, c=chunk: (layer, c, i // tiles_per_seq, 0, col_of(i, j))))
    return specs


def _modulate_rows(x_ref, g_ref, sh_ref, sc_ref, h_scr, per_token):
    tm = x_ref.shape[0]
    chunk = min(ROW_CHUNK, tm)

    def body(c, carry):
        rows = pl.ds(pl.multiple_of(c * chunk, chunk), chunk)
        sh = sh_ref[rows, :] if per_token else sh_ref[...]
        sc = sc_ref[rows, :] if per_token else sc_ref[...]
        h_scr[rows, :] = _modulate(x_ref[rows, :], g_ref[...], sh, sc).astype(BF16)
        return carry

    lax.fori_loop(0, tm // chunk, body, 0)


def _ffn_kernel(x_ref, sh_ref, sc_ref, gt_ref, g_ref, w1_ref, w3_ref, w2_ref, gf_ref, o_ref,
                h_scr, act_scr, *, final_norm, per_token):
    j = pl.program_id(1)
    last = pl.num_programs(1) - 1

    @pl.when(j == 0)
    def _():
        _modulate_rows(x_ref, g_ref, sh_ref, sc_ref, h_scr, per_token)
        o_ref[...] = jnp.zeros_like(o_ref)
        act_scr[...] = jnp.zeros_like(act_scr)

    @pl.when(j < last)
    def _():
        h = h_scr[...]
        a = jnp.dot(h, w1_ref[...].astype(BF16), preferred_element_type=F32)
        b = jnp.dot(h, w3_ref[...].astype(BF16), preferred_element_type=F32)
        o_ref[...] += jnp.dot(act_scr[...], w2_ref[...].astype(BF16), preferred_element_type=F32)
        act_scr[...] = (jax.nn.silu(a) * b).astype(BF16)

    @pl.when(j == last)
    def _():
        acc = o_ref[...] + jnp.dot(act_scr[...], w2_ref[...].astype(BF16), preferred_element_type=F32)
        y = x_ref[...] + 0.5 * gt_ref[...] * acc
        if final_norm:
            y = _rmsnorm(y, gf_ref[...])
        o_ref[...] = y


def _ffn(x, mod, g_norm4, w1, w3, w2, g_final, *, layer, which, per_token, tm, tiles_per_seq, final_norm):
    n_tok = x.shape[0]
    n_ff = D_FF // FFN_TF
    chunk0 = 0 if which == 0 else 6
    norm_idx = 0 if which == 0 else 2
    zero_col = lambda i, j: 0
    in_specs = [pl.BlockSpec((tm, D_MODEL), lambda i, j: (i, 0), pipeline_mode=pl.Buffered(1))]
    in_specs += _mod_specs(per_token, tm, layer, (chunk0, chunk0 + 1, chunk0 + 2), tiles_per_seq, D_MODEL, zero_col)
    in_specs += [
        pl.BlockSpec((None, None, 1, D_MODEL), lambda i, j: (layer, norm_idx, 0, 0)),
        pl.BlockSpec((None, None, D_MODEL, FFN_TF), lambda i, j: (layer, which, 0, jnp.minimum(j, n_ff - 1))),
        pl.BlockSpec((None, None, D_MODEL, FFN_TF), lambda i, j: (layer, which, 0, jnp.minimum(j, n_ff - 1))),
        pl.BlockSpec((None, None, FFN_TF, D_MODEL), lambda i, j: (layer, which, jnp.maximum(j - 1, 0), 0)),
        pl.BlockSpec((1, D_MODEL), lambda i, j: (0, 0)),
    ]
    return pl.pallas_call(
        functools.partial(_ffn_kernel, final_norm=final_norm, per_token=per_token),
        grid=(n_tok // tm, n_ff + 1),
        in_specs=in_specs,
        out_specs=pl.BlockSpec((tm, D_MODEL), lambda i, j: (i, 0)),
        out_shape=jax.ShapeDtypeStruct((n_tok, D_MODEL), F32),
        scratch_shapes=[pltpu.VMEM((tm, D_MODEL), BF16), pltpu.VMEM((tm, FFN_TF), BF16)],
        compiler_params=_params(("parallel", "arbitrary")),
        name="ffn",
    )(x, mod, mod, mod, g_norm4, w1, w3, w2, g_final)


_PROJ_HALVES = POOL_W // PROJ_TN


def _proj_kernel(*refs, per_token):
    x_ref, sh_ref, sc_ref, g_ref, w_ref = refs[:5]
    u_ref, q_ref, k_ref, v_ref, h_scr = refs[-5:]
    j = pl.program_id(1)

    @pl.when(j == 0)
    def _():
        _modulate_rows(x_ref, g_ref, sh_ref, sc_ref, h_scr, per_token)

    for idx, ref in enumerate((u_ref, q_ref, k_ref, v_ref)):
        @pl.when(j // _PROJ_HALVES == idx)
        def _(ref=ref):
            res = jnp.dot(h_scr[...], w_ref[...].astype(BF16), preferred_element_type=F32)
            ref[...] = res.astype(ref.dtype)


def _proj(x, mod, g_norm4, w_in, kv_prev, *, layer, depth, per_token, tm, tiles_per_seq):
    n_tok = x.shape[0]
    zero_col = lambda i, j: 0

    def half_of(idx):
        return lambda i, j: jnp.clip(j - idx * _PROJ_HALVES, 0, _PROJ_HALVES - 1)

    in_specs = [pl.BlockSpec((tm, D_MODEL), lambda i, j: (i, 0))]
    in_specs += _mod_specs(per_token, tm, layer, (3, 4), tiles_per_seq, D_MODEL, zero_col)
    in_specs += [
        pl.BlockSpec((None, None, 1, D_MODEL), lambda i, j: (layer, 1, 0, 0)),
        pl.BlockSpec((None, D_MODEL, PROJ_TN), lambda i, j: (layer, 0, j)),
    ]
    operands = [x, mod, mod, g_norm4, w_in]
    aliases = {}
    if kv_prev is not None:
        in_specs += [pl.BlockSpec(memory_space=pl.ANY), pl.BlockSpec(memory_space=pl.ANY)]
        aliases = {len(operands): 2, len(operands) + 1: 3}
        operands += list(kv_prev)
    out_specs = [
        pl.BlockSpec((tm, PROJ_TN), lambda i, j: (i, half_of(0)(i, j))),
        pl.BlockSpec((tm, PROJ_TN), lambda i, j: (i, half_of(1)(i, j))),
        pl.BlockSpec((None, tm, PROJ_TN), lambda i, j: (layer, i, half_of(2)(i, j))),
        pl.BlockSpec((None, tm, PROJ_TN), lambda i, j: (layer, i, half_of(3)(i, j))),
    ]
    out_shape = [
        jax.ShapeDtypeStruct((n_tok, POOL_W), F32),
        jax.ShapeDtypeStruct((n_tok, ATT_W), BF16),
        jax.ShapeDtypeStruct((depth, n_tok, ATT_W), F32),
        jax.ShapeDtypeStruct((depth, n_tok, ATT_W), F32),
    ]
    return pl.pallas_call(
        functools.partial(_proj_kernel, per_token=per_token),
        grid=(n_tok // tm, PROJ_W // PROJ_TN),
        in_specs=in_specs,
        out_specs=out_specs,
        out_shape=out_shape,
        input_output_aliases=aliases,
        scratch_shapes=[pltpu.VMEM((tm, D_MODEL), BF16)],
        compiler_params=_params(("parallel", "arbitrary")),
        name="proj",
    )(*operands)


def _attn_kernel(bias_ref, q_ref, k_ref, v_ref, tri_ref, gh_ref, o_ref):
    group = pl.program_id(1)
    qi = pl.program_id(2)
    tri = tri_ref[...]
    row = lax.broadcasted_iota(jnp.int32, (ATT_T, ATT_T), 0)
    col = lax.broadcasted_iota(jnp.int32, (ATT_T, ATT_T), 1)
    below_diag = col < row
    heads = [slice(g * HEAD_DIM, (g + 1) * HEAD_DIM) for g in range(ATT_G)]
    qs = [q_ref[:, cols] for cols in heads]
    biases = [bias_ref[group * ATT_G + g] for g in range(ATT_G)]

    def block(kb, state, diagonal):
        start = pl.multiple_of(kb * ATT_T, ATT_T)
        zs = []
        for g, cols in enumerate(heads):
            kblk = k_ref[pl.ds(start, ATT_T), cols].astype(BF16)
            z = lax.dot_general(qs[g], kblk, (((1,), (1,)), ((), ())), preferred_element_type=F32)
            zs.append(z * ATT_SCALE + biases[g])
        log_hits, totals, parts = [], [], []
        for z in zs:
            lf = -_softplus(z)
            if diagonal:
                lf = jnp.where(below_diag, lf, 0.0)
            log_hits.append(z + lf)
            totals.append(jnp.sum(lf, axis=-1, keepdims=True))
            parts.extend(_split_bf16(lf))
        within_all = jnp.dot(jnp.concatenate(parts, axis=0), tri, preferred_element_type=F32)
        out = []
        for g, cols in enumerate(heads):
            carry, acc = state[g]
            rows = 2 * g * ATT_T
            within = within_all[rows:rows + ATT_T] + within_all[rows + ATT_T:rows + 2 * ATT_T]
            a = jnp.exp(log_hits[g] + (within + carry))
            if diagonal:
                a = jnp.where(below_diag, a, 0.0)
            vblk = v_ref[pl.ds(start, ATT_T), cols].astype(BF16)
            acc = acc + jnp.dot(a.astype(BF16), vblk, preferred_element_type=F32)
            out.append((carry + totals[g], acc))
        return tuple(out)

    state = tuple((jnp.zeros((ATT_T, 1), F32), jnp.zeros((ATT_T, HEAD_DIM), F32)) for _ in heads)
    state = block(qi, state, True)
    state = lax.fori_loop(0, qi, lambda i, s: block(qi - 1 - i, s, False), state)
    for g, cols in enumerate(heads):
        o_ref[:, cols] = _rmsnorm(state[g][1], gh_ref[g:g + 1, :]).astype(BF16)


def _strict_upper_ones(n):
    j = lax.broadcasted_iota(jnp.int32, (n, n), 0)
    s = lax.broadcasted_iota(jnp.int32, (n, n), 1)
    return (j > s).astype(BF16)


def _attn_prompt(q, k_all, v_all, sb_bias_l, g_head_l, *, layer, batch, seq):
    nq = seq // ATT_T
    gw = ATT_G * HEAD_DIM
    return pl.pallas_call(
        _attn_kernel,
        grid=(batch, N_HEADS // ATT_G, nq),
        in_specs=[
            pl.BlockSpec(memory_space=pltpu.SMEM),
            pl.BlockSpec((ATT_T, gw), lambda b, h, i: (b * nq + i, h)),
            pl.BlockSpec((None, seq, gw), lambda b, h, i: (layer, b, h)),
            pl.BlockSpec((None, seq, gw), lambda b, h, i: (layer, b, h)),
            pl.BlockSpec((ATT_T, ATT_T), lambda b, h, i: (0, 0)),
            pl.BlockSpec((None, ATT_G, HEAD_DIM), lambda b, h, i: (h, 0, 0)),
        ],
        out_specs=pl.BlockSpec((ATT_T, gw), lambda b, h, i: (b * nq + i, h)),
        out_shape=jax.ShapeDtypeStruct((batch * seq, ATT_W), BF16),
        compiler_params=_params(("parallel", "parallel", "arbitrary")),
        name="attn_prompt",
    )(sb_bias_l, q, k_all, v_all, _strict_upper_ones(ATT_T),
      g_head_l.reshape(N_HEADS // ATT_G, ATT_G, HEAD_DIM))


def _attn_sample_kernel(pt_ref, bias_ref, q_ref, *refs, n_pages):
    del pt_ref
    k_refs = refs[:n_pages]
    v_refs = refs[n_pages:2 * n_pages]
    tri_ref, gh_ref, o_ref = refs[2 * n_pages:]
    width = PAGE_SIZE * N_HEADS
    q = q_ref[...].astype(BF16)
    bias = bias_ref[...]
    lane = lax.broadcasted_iota(jnp.int32, (N_HEADS, width), 1)
    sub = lax.broadcasted_iota(jnp.int32, (N_HEADS, width), 0)
    own = jnp.bitwise_and(lane, N_HEADS - 1) == sub

    zs, lfs = [], []
    for p in range(n_pages):
        z = lax.dot_general(q, k_refs[p][...].astype(BF16), (((1,), (1,)), ((), ())),
                            preferred_element_type=F32)
        z = z * ATT_SCALE + bias
        zs.append(z)
        lfs.append(jnp.where(own, -_softplus(z), 0.0))

    lf_all = jnp.concatenate(lfs, axis=0)
    hi, lo = _split_bf16(lf_all)
    tri = tri_ref[...]
    within = (jnp.dot(hi, tri, preferred_element_type=F32)
              + jnp.dot(lo, tri, preferred_element_type=F32))
    totals = jnp.sum(lf_all, axis=-1, keepdims=True)

    carry = jnp.zeros((N_HEADS, 1), F32)
    acc = jnp.zeros((N_HEADS, HEAD_DIM), F32)
    for p in reversed(range(n_pages)):
        rows = slice(p * N_HEADS, (p + 1) * N_HEADS)
        a = jnp.exp(zs[p] + lfs[p] + (within[rows] + carry))
        a = jnp.where(own, a, 0.0).astype(BF16)
        acc = acc + jnp.dot(a, v_refs[p][...].astype(BF16), preferred_element_type=F32)
        carry = carry + totals[rows]
    o_ref[...] = _rmsnorm(acc, gh_ref[...])


def _attn_sample(q_s, cache_k, cache_v, page_table, sb_bias_l, g_head_l, *, layer):
    dec_b, n_pages = page_table.shape
    n_phys = cache_k.shape[1]
    width = PAGE_SIZE * N_HEADS
    ck = cache_k.reshape(cache_k.shape[0], n_phys, width, HEAD_DIM)
    cv = cache_v.reshape(cache_v.shape[0], n_phys, width, HEAD_DIM)

    def page_spec(p):
        return pl.BlockSpec((None, None, width, HEAD_DIM), lambda b, pt, p=p: (layer, pt[b, p], 0, 0))

    in_specs = [
        pl.BlockSpec((N_HEADS, 1), lambda b, pt: (0, 0)),
        pl.BlockSpec((None, N_HEADS, HEAD_DIM), lambda b, pt: (b, 0, 0)),
    ]
    in_specs += [page_spec(p) for p in range(n_pages)]
    in_specs += [page_spec(p) for p in range(n_pages)]
    in_specs += [
        pl.BlockSpec((width, width), lambda b, pt: (0, 0)),
        pl.BlockSpec((N_HEADS, HEAD_DIM), lambda b, pt: (0, 0)),
    ]
    grid_spec = pltpu.PrefetchScalarGridSpec(
        num_scalar_prefetch=1,
        grid=(dec_b,),
        in_specs=in_specs,
        out_specs=pl.BlockSpec((None, N_HEADS, HEAD_DIM), lambda b, pt: (b, 0, 0)),
    )
    return pl.pallas_call(
        functools.partial(_attn_sample_kernel, n_pages=n_pages),
        grid_spec=grid_spec,
        out_shape=jax.ShapeDtypeStruct((dec_b, N_HEADS, HEAD_DIM), F32),
        compiler_params=_params(("arbitrary",)),
        name="attn_sample",
    )(page_table, sb_bias_l.reshape(N_HEADS, 1), q_s, *([ck] * n_pages), *([cv] * n_pages),
      _strict_upper_ones(width), g_head_l)


def _pool_to_cat(diffs, wp_ref, ps_ref, sb_ref, cat_scr):
    for g in range(len(POOL_WINDOWS)):
        cols = slice(g * POOL_GC, (g + 1) * POOL_GC)
        mixed = jnp.dot(diffs[g].astype(BF16), wp_ref[g].astype(BF16), preferred_element_type=F32)
        cat_scr[:, cols] = (mixed * ps_ref[:, cols]).astype(BF16)
    cat_scr[:, POOL_W:] = sb_ref[...].astype(BF16)


def _mixout_prompt_kernel(u_ref, halo_ref, sb_ref, x_ref, gt_ref, wp_ref, ps_ref, wo_ref, o_ref,
                          ext_scr, cat_scr, *, tm, tiles_per_seq):
    i = pl.program_id(0)

    @pl.when(pl.program_id(1) == 0)
    def _():
        first = (i % tiles_per_seq) == 0
        ext_scr[0:HALO, :] = jnp.where(first, 0.0, halo_ref[...])
        ext_scr[HALO:, :] = u_ref[...]
        pos = (i % tiles_per_seq) * tm + lax.broadcasted_iota(jnp.int32, (tm, 1), 0)
        diffs = []
        for g, w in enumerate(POOL_WINDOWS):
            cols = slice(g * POOL_GC, (g + 1) * POOL_GC)
            total = ext_scr[HALO:HALO + tm, cols]
            for d in range(1, w):
                total = total + ext_scr[HALO - d:HALO - d + tm, cols]
            cnt = jnp.minimum(pos + 1, w).astype(F32)
            diffs.append(total / cnt - ext_scr[HALO:HALO + tm, cols])
        _pool_to_cat(diffs, wp_ref, ps_ref, sb_ref, cat_scr)

    width = o_ref.shape[1]
    for c0 in range(0, width, OUT_TN):
        cols = slice(c0, min(c0 + OUT_TN, width))
        mixed = jnp.dot(cat_scr[...], wo_ref[:, cols].astype(BF16), preferred_element_type=F32)
        o_ref[:, cols] = x_ref[:, cols] + gt_ref[:, cols] * mixed


def _mixout_sample_kernel(ue_ref, sb_ref, x_ref, gt_ref, wp_ref, ps_ref, wo_ref, o_ref, cat_scr):
    @pl.when(pl.program_id(1) == 0)
    def _():
        diffs = []
        for g, w in enumerate(POOL_WINDOWS):
            cols = slice(g * POOL_GC, (g + 1) * POOL_GC)
            total = ue_ref[POOL_BUF, :, cols]
            for d in range(1, w):
                total = total + ue_ref[POOL_BUF - d, :, cols]
            diffs.append(total / float(w) - ue_ref[POOL_BUF, :, cols])
        _pool_to_cat(diffs, wp_ref, ps_ref, sb_ref, cat_scr)

    mixed = jnp.dot(cat_scr[...], wo_ref[...].astype(BF16), preferred_element_type=F32)
    o_ref[...] = x_ref[...] + gt_ref[...] * mixed


def _mixout_prompt(u, sb, x, mod, w_pool, pool_scale3, w_out, *, layer, tm, tiles_per_seq):
    n_tok = x.shape[0]
    halo_blocks = tm // HALO
    in_specs = [
        pl.BlockSpec((tm, POOL_W), lambda i, j: (i, 0)),
        pl.BlockSpec((HALO, POOL_W), lambda i, j: (jnp.maximum(i * halo_blocks - 1, 0), 0)),
        pl.BlockSpec((tm, ATT_W), lambda i, j: (i, 0)),
        pl.BlockSpec((tm, D_MODEL), lambda i, j: (i, 0)),
    ]
    in_specs += _mod_specs(False, tm, layer, (5,), tiles_per_seq, D_MODEL, lambda i, j: 0)
    in_specs += [
        pl.BlockSpec((None, len(POOL_WINDOWS), POOL_GC, POOL_GC), lambda i, j: (layer, 0, 0, 0)),
        pl.BlockSpec((None, 1, POOL_W), lambda i, j: (layer, 0, 0)),
        pl.BlockSpec((None, D_MODEL, D_MODEL), lambda i, j: (layer, 0, 0), pipeline_mode=pl.Buffered(1)),
    ]
    return pl.pallas_call(
        functools.partial(_mixout_prompt_kernel, tm=tm, tiles_per_seq=tiles_per_seq),
        grid=(n_tok // tm, 1),
        in_specs=in_specs,
        out_specs=pl.BlockSpec((tm, D_MODEL), lambda i, j: (i, 0)),
        out_shape=jax.ShapeDtypeStruct((n_tok, D_MODEL), F32),
        scratch_shapes=[pltpu.VMEM((tm + HALO, POOL_W), F32), pltpu.VMEM((tm, D_MODEL), BF16)],
        compiler_params=_params(("parallel", "arbitrary")),
        name="mixout_prompt",
    )(u, u, sb, x, mod, w_pool, pool_scale3, w_out)


def _mixout_sample(u_ext, sb, x, mod, w_pool, pool_scale3, w_out, *, layer):
    n_tok = x.shape[0]
    in_specs = [
        pl.BlockSpec((POOL_BUF + 1, n_tok, POOL_W), lambda i, j: (0, 0, 0)),
        pl.BlockSpec((n_tok, ATT_W), lambda i, j: (0, 0)),
        pl.BlockSpec((n_tok, OUT_TN), lambda i, j: (0, j)),
    ]
    in_specs += _mod_specs(True, n_tok, layer, (5,), 1, OUT_TN, lambda i, j: j)
    in_specs += [
        pl.BlockSpec((None, len(POOL_WINDOWS), POOL_GC, POOL_GC), lambda i, j: (layer, 0, 0, 0)),
        pl.BlockSpec((None, 1, POOL_W), lambda i, j: (layer, 0, 0)),
        pl.BlockSpec((None, D_MODEL, OUT_TN), lambda i, j: (layer, 0, j)),
    ]
    return pl.pallas_call(
        _mixout_sample_kernel,
        grid=(1, D_MODEL // OUT_TN),
        in_specs=in_specs,
        out_specs=pl.BlockSpec((n_tok, OUT_TN), lambda i, j: (0, j)),
        out_shape=jax.ShapeDtypeStruct((n_tok, D_MODEL), F32),
        scratch_shapes=[pltpu.VMEM((n_tok, D_MODEL), BF16)],
        compiler_params=_params(("arbitrary", "arbitrary")),
        name="mixout_sample",
    )(u_ext, sb, x, mod, w_pool, pool_scale3, w_out)


def kernel(x_prompt, x_sample, c_prompt, c_sample, cache_k, cache_v, state_pool, page_table,
           w_ada, b_ada, g_norm, w1, w3, w2, w_in, w_pool, pool_scale, g_head, sb_bias, w_out, g_final):
    batch, seq, _ = x_prompt.shape
    dec_b = x_sample.shape[0]
    depth = w_ada.shape[0]
    tm_p = TM_PROMPT
    tiles_per_seq = seq // tm_p

    pad = (-(dec_b + batch)) % 16
    c_all = jnp.concatenate([c_sample, c_prompt, jnp.zeros((pad, D_MODEL), F32)], axis=0)
    mod = _ada(c_all, w_ada, b_ada)
    mod_s = mod[:, :, :dec_b]
    mod_p = mod[:, :, dec_b:dec_b + batch].reshape(depth, N_MOD, batch, 1, D_MODEL)

    g_norm4 = g_norm.reshape(depth, 3, 1, D_MODEL)
    pool_scale3 = pool_scale.reshape(depth, 1, POOL_W)
    g_final2 = g_final.reshape(1, D_MODEL)

    xp = x_prompt.reshape(batch * seq, D_MODEL)
    xs = x_sample.reshape(dec_b, D_MODEL)
    kv_p = kv_s = None
    pp_l, ps_l = [], []
    for l in range(depth):
        last = l == depth - 1
        ffn_p = functools.partial(_ffn, mod=mod_p, g_norm4=g_norm4, w1=w1, w3=w3, w2=w2, g_final=g_final2,
                                  layer=l, per_token=False, tm=tm_p, tiles_per_seq=tiles_per_seq)
        xp = ffn_p(xp, which=0, final_norm=False)
        u_p, q_p, k_p, v_p = _proj(xp, mod_p, g_norm4, w_in, kv_p, layer=l, depth=depth,
                                   per_token=False, tm=tm_p, tiles_per_seq=tiles_per_seq)
        kv_p = (k_p, v_p)
        sb_p = _attn_prompt(q_p, k_p, v_p, sb_bias[l], g_head[l], layer=l, batch=batch, seq=seq)
        xp = _mixout_prompt(u_p, sb_p, xp, mod_p, w_pool, pool_scale3, w_out,
                            layer=l, tm=TM_MIX, tiles_per_seq=seq // TM_MIX)
        xp = ffn_p(xp, which=1, final_norm=last)
        pp_l.append(u_p.reshape(batch, seq, POOL_W)[:, seq - POOL_BUF:])

        ffn_s = functools.partial(_ffn, mod=mod_s, g_norm4=g_norm4, w1=w1, w3=w3, w2=w2, g_final=g_final2,
                                  layer=l, per_token=True, tm=dec_b, tiles_per_seq=1)
        xs = ffn_s(xs, which=0, final_norm=False)
        u_s, q_s, k_s, v_s = _proj(xs, mod_s, g_norm4, w_in, kv_s, layer=l, depth=depth,
                                   per_token=True, tm=dec_b, tiles_per_seq=1)
        kv_s = (k_s, v_s)
        q_s3 = q_s.astype(F32).reshape(dec_b, N_HEADS, HEAD_DIM)
        sb_s = _attn_sample(q_s3, cache_k, cache_v, page_table, sb_bias[l], g_head[l], layer=l)
        u_ext = jnp.concatenate([jnp.transpose(state_pool[l], (1, 0, 2)), u_s[None]], axis=0)
        xs = _mixout_sample(u_ext, sb_s.reshape(dec_b, ATT_W), xs, mod_s, w_pool, pool_scale3, w_out, layer=l)
        xs = ffn_s(xs, which=1, final_norm=last)
        ps_l.append(jnp.concatenate([state_pool[l][:, 1:], u_s[:, None, :]], axis=1))

    y_prompt = xp.reshape(batch, seq, D_MODEL)
    y_sample = xs.reshape(dec_b, 1, D_MODEL)
    kv5 = lambda a, rows, t: a.reshape(depth, rows, t, N_HEADS, HEAD_DIM)
    return (y_prompt, y_sample, kv5(kv_p[0], batch, seq), kv5(kv_p[1], batch, seq), jnp.stack(pp_l),
            kv5(kv_s[0], dec_b, 1), kv5(kv_s[1], dec_b, 1), jnp.stack(ps_l))
```

```python
import functools

import jax
import jax.numpy as jnp
from jax import lax
from jax.experimental import pallas as pl
from jax.experimental.pallas import tpu as pltpu

D_MODEL = 2048
POOL_W = 1024
ATT_W = 1024
HEAD_DIM = 128
N_HEADS = 8
POOL_WINDOWS = (2, 4, 8, 16)
POOL_GC = 256
POOL_BUF = 15
D_FF = 5632
N_MOD = 9
EPS = 1e-6
PAGE_SIZE = 128
PROJ_W = POOL_W + 3 * ATT_W
ATT_SCALE = HEAD_DIM ** -0.5

F32 = jnp.float32
BF16 = jnp.bfloat16

VMEM_LIMIT_BYTES = 56 * 1024 * 1024

TM_PROMPT = 1024
TM_MIX = 512
FFN_TF = 256
PROJ_TN = 512
OUT_TN = 512
ADA_TN = 1024
ATT_T = 256
ATT_G = 8
ROW_CHUNK = 128
HALO = 16


def _params(sem):
    return pltpu.CompilerParams(dimension_semantics=sem, vmem_limit_bytes=VMEM_LIMIT_BYTES)


def _modulate(x, g, shift, scale):
    ms = jnp.mean(x * x, axis=-1, keepdims=True)
    y = x * lax.rsqrt(ms + EPS) * g
    return y * (1.0 + scale) + shift


def _rmsnorm(x, g):
    ms = jnp.mean(x * x, axis=-1, keepdims=True)
    return x * lax.rsqrt(ms + EPS) * g


def _softplus(z):
    return jnp.maximum(z, 0.0) + jnp.log(1.0 + jnp.exp(-jnp.abs(z)))


def _split_bf16(x):
    hi = x.astype(BF16)
    lo = (x - hi.astype(F32)).astype(BF16)
    return hi, lo


def _ada_kernel(c_ref, w_ref, b_ref, os_ref, op_ref):
    s = jax.nn.silu(c_ref[...]).astype(BF16)
    mod = jnp.dot(s, w_ref[...].astype(BF16), preferred_element_type=F32) + b_ref[...]
    n_s = os_ref.shape[0]
    os_ref[...] = mod[:n_s]
    op_ref[...] = mod[n_s:]


def _ada(c_all, n_sample, w_ada, b_ada):
    depth = w_ada.shape[0]
    rows = c_all.shape[0]
    per_chunk = D_MODEL // ADA_TN
    out_map = lambda l, n: (l, n // per_chunk, 0, n % per_chunk)
    return pl.pallas_call(
        _ada_kernel,
        grid=(depth, N_MOD * per_chunk),
        in_specs=[
            pl.BlockSpec((rows, D_MODEL), lambda l, n: (0, 0)),
            pl.BlockSpec((None, D_MODEL, ADA_TN), lambda l, n: (l, 0, n)),
            pl.BlockSpec((None, 1, ADA_TN), lambda l, n: (l, 0, n)),
        ],
        out_specs=[pl.BlockSpec((None, None, n_sample, ADA_TN), out_map),
                   pl.BlockSpec((None, None, rows - n_sample, ADA_TN), out_map)],
        out_shape=[jax.ShapeDtypeStruct((depth, N_MOD, n_sample, D_MODEL), F32),
                   jax.ShapeDtypeStruct((depth, N_MOD, rows - n_sample, D_MODEL), F32)],
        compiler_params=_params(("parallel", "arbitrary")),
        name="ada",
    )(c_all, w_ada, b_ada.reshape(depth, 1, N_MOD * D_MODEL))


def _mod_specs(per_token, tm, layer, chunks, tiles_per_seq, width, col_of):
    specs = []
    for chunk in chunks:
        if per_token:
            specs.append(pl.BlockSpec((None, None, tm, width),
                                      lambda i, j, c=chunk: (layer, c, i, col_of(i, j))))
        else:
            specs.append(pl.BlockSpec((None, None, None, 1, width),
                                      lambda i, j, c=chunk: (layer, c, i // tiles_per_seq, 0, col_of(i, j))))
    return specs


def _modulate_rows(x_ref, g_ref, sh_ref, sc_ref, h_scr, per_token):
    tm = x_ref.shape[0]
    chunk = min(ROW_CHUNK, tm)

    def body(c, carry):
        rows = pl.ds(pl.multiple_of(c * chunk, chunk), chunk)
        sh = sh_ref[rows, :] if per_token else sh_ref[...]
        sc = sc_ref[rows, :] if per_token else sc_ref[...]
        h_scr[rows, :] = _modulate(x_ref[rows, :], g_ref[...], sh, sc).astype(BF16)
        return carry

    lax.fori_loop(0, tm // chunk, body, 0)


def _ffn_kernel(xp_ref, xs_ref, shp_ref, scp_ref, gtp_ref, shs_ref, scs_ref, gts_ref, g_ref,
                w1_ref, w3_ref, w2_ref, gf_ref, op_ref, os_ref, h_scr, act_scr, acc_scr, *, final_norm):
    j = pl.program_id(1)
    last = pl.num_programs(1) - 1
    tm = xp_ref.shape[0]

    @pl.when(j == 0)
    def _():
        _modulate_rows(xp_ref, g_ref, shp_ref, scp_ref, h_scr, False)
        h_scr[tm:, :] = _modulate(xs_ref[...], g_ref[...], shs_ref[...], scs_ref[...]).astype(BF16)
        acc_scr[...] = jnp.zeros_like(acc_scr)
        act_scr[...] = jnp.zeros_like(act_scr)

    @pl.when(j < last)
    def _():
        h = h_scr[...]
        a = jnp.dot(h, w1_ref[...].astype(BF16), preferred_element_type=F32)
        b = jnp.dot(h, w3_ref[...].astype(BF16), preferred_element_type=F32)
        acc_scr[...] += jnp.dot(act_scr[...], w2_ref[...].astype(BF16), preferred_element_type=F32)
        act_scr[...] = (jax.nn.silu(a) * b).astype(BF16)

    @pl.when(j == last)
    def _():
        acc_scr[...] += jnp.dot(act_scr[...], w2_ref[...].astype(BF16), preferred_element_type=F32)
        yp = xp_ref[...] + 0.5 * gtp_ref[...] * acc_scr[:tm, :]
        ys = xs_ref[...] + 0.5 * gts_ref[...] * acc_scr[tm:, :]
        if final_norm:
            yp = _rmsnorm(yp, gf_ref[...])
            ys = _rmsnorm(ys, gf_ref[...])
        op_ref[...] = yp
        os_ref[...] = ys


def _ffn(xp, xs, mod_p, mod_s, g_norm4, w1, w3, w2, g_final, *, layer, which, tm, tiles_per_seq, final_norm):
    n_tok = xp.shape[0]
    n_tiles = n_tok // tm
    rs = xs.shape[0] // n_tiles
    assert rs * n_tiles == xs.shape[0] and rs % 8 == 0
    n_ff = D_FF // FFN_TF
    chunk0 = 0 if which == 0 else 6
    norm_idx = 0 if which == 0 else 2
    chunks = (chunk0, chunk0 + 1, chunk0 + 2)
    zero_col = lambda i, j: 0
    in_specs = [
        pl.BlockSpec((tm, D_MODEL), lambda i, j: (i, 0), pipeline_mode=pl.Buffered(1)),
        pl.BlockSpec((rs, D_MODEL), lambda i, j: (i, 0)),
    ]
    in_specs += _mod_specs(False, tm, layer, chunks, tiles_per_seq, D_MODEL, zero_col)
    in_specs += _mod_specs(True, rs, layer, chunks, 1, D_MODEL, zero_col)
    in_specs += [
        pl.BlockSpec((None, None, 1, D_MODEL), lambda i, j: (layer, norm_idx, 0, 0)),
        pl.BlockSpec((None, None, D_MODEL, FFN_TF), lambda i, j: (layer, which, 0, jnp.minimum(j, n_ff - 1))),
        pl.BlockSpec((None, None, D_MODEL, FFN_TF), lambda i, j: (layer, which, 0, jnp.minimum(j, n_ff - 1))),
        pl.BlockSpec((None, None, FFN_TF, D_MODEL), lambda i, j: (layer, which, jnp.maximum(j - 1, 0), 0)),
        pl.BlockSpec((1, D_MODEL), lambda i, j: (0, 0)),
    ]
    return pl.pallas_call(
        functools.partial(_ffn_kernel, final_norm=final_norm),
        grid=(n_tiles, n_ff + 1),
        in_specs=in_specs,
        out_specs=[pl.BlockSpec((tm, D_MODEL), lambda i, j: (i, 0)),
                   pl.BlockSpec((rs, D_MODEL), lambda i, j: (i, 0))],
        out_shape=[jax.ShapeDtypeStruct((n_tok, D_MODEL), F32),
                   jax.ShapeDtypeStruct(xs.shape, F32)],
        scratch_shapes=[pltpu.VMEM((tm + rs, D_MODEL), BF16), pltpu.VMEM((tm + rs, FFN_TF), BF16),
                        pltpu.VMEM((tm + rs, D_MODEL), F32)],
        compiler_params=_params(("parallel", "arbitrary")),
        name="ffn",
    )(xp, xs, mod_p, mod_p, mod_p, mod_s, mod_s, mod_s, g_norm4, w1, w3, w2, g_final)


_PROJ_HALVES = POOL_W // PROJ_TN


def _proj_kernel(*refs, per_token):
    x_ref, sh_ref, sc_ref, g_ref, w_ref = refs[:5]
    u_ref, q_ref, k_ref, v_ref, h_scr = refs[-5:]
    j = pl.program_id(1)

    @pl.when(j == 0)
    def _():
        _modulate_rows(x_ref, g_ref, sh_ref, sc_ref, h_scr, per_token)

    for idx, ref in enumerate((u_ref, q_ref, k_ref, v_ref)):
        @pl.when(j // _PROJ_HALVES == idx)
        def _(ref=ref):
            res = jnp.dot(h_scr[...], w_ref[...].astype(BF16), preferred_element_type=F32)
            ref[...] = res.astype(ref.dtype)


def _proj(x, mod, g_norm4, w_in, kv_prev, *, layer, depth, per_token, tm, tiles_per_seq):
    n_tok = x.shape[0]
    zero_col = lambda i, j: 0

    def half_of(idx):
        return lambda i, j: jnp.clip(j - idx * _PROJ_HALVES, 0, _PROJ_HALVES - 1)

    in_specs = [pl.BlockSpec((tm, D_MODEL), lambda i, j: (i, 0))]
    in_specs += _mod_specs(per_token, tm, layer, (3, 4), tiles_per_seq, D_MODEL, zero_col)
    in_specs += [
        pl.BlockSpec((None, None, 1, D_MODEL), lambda i, j: (layer, 1, 0, 0)),
        pl.BlockSpec((None, D_MODEL, PROJ_TN), lambda i, j: (layer, 0, j)),
    ]
    operands = [x, mod, mod, g_norm4, w_in]
    aliases = {}
    if kv_prev is not None:
        in_specs += [pl.BlockSpec(memory_space=pl.ANY), pl.BlockSpec(memory_space=pl.ANY)]
        aliases = {len(operands): 2, len(operands) + 1: 3}
        operands += list(kv_prev)
    out_specs = [
        pl.BlockSpec((tm, PROJ_TN), lambda i, j: (i, half_of(0)(i, j))),
        pl.BlockSpec((tm, PROJ_TN), lambda i, j: (i, half_of(1)(i, j))),
        pl.BlockSpec((None, tm, PROJ_TN), lambda i, j: (layer, i, half_of(2)(i, j))),
        pl.BlockSpec((None, tm, PROJ_TN), lambda i, j: (layer, i, half_of(3)(i, j))),
    ]
    out_shape = [
        jax.ShapeDtypeStruct((n_tok, POOL_W), F32),
        jax.ShapeDtypeStruct((n_tok, ATT_W), BF16),
        jax.ShapeDtypeStruct((depth, n_tok, ATT_W), F32),
        jax.ShapeDtypeStruct((depth, n_tok, ATT_W), F32),
    ]
    return pl.pallas_call(
        functools.partial(_proj_kernel, per_token=per_token),
        grid=(n_tok // tm, PROJ_W // PROJ_TN),
        in_specs=in_specs,
        out_specs=out_specs,
        out_shape=out_shape,
        input_output_aliases=aliases,
        scratch_shapes=[pltpu.VMEM((tm, D_MODEL), BF16)],
        compiler_params=_params(("parallel", "arbitrary")),
        name="proj",
    )(*operands)


def _attn_kernel(bias_ref, q_ref, k_ref, v_ref, tri_ref, gh_ref, o_ref):
    group = pl.program_id(1)
    qi = pl.program_id(2)
    tri = tri_ref[...]
    row = lax.broadcasted_iota(jnp.int32, (ATT_T, ATT_T), 0)
    col = lax.broadcasted_iota(jnp.int32, (ATT_T, ATT_T), 1)
    below_diag = col < row
    heads = [slice(g * HEAD_DIM, (g + 1) * HEAD_DIM) for g in range(ATT_G)]
    qs = [q_ref[:, cols] for cols in heads]
    biases = [bias_ref[group * ATT_G + g] for g in range(ATT_G)]

    def block(kb, state, diagonal):
        start = pl.multiple_of(kb * ATT_T, ATT_T)
        zs = []
        for g, cols in enumerate(heads):
            kblk = k_ref[pl.ds(start, ATT_T), cols].astype(BF16)
            z = lax.dot_general(qs[g], kblk, (((1,), (1,)), ((), ())), preferred_element_type=F32)
            zs.append(z * ATT_SCALE + biases[g])
        log_hits, totals, parts = [], [], []
        for z in zs:
            lf = -_softplus(z)
            if diagonal:
                lf = jnp.where(below_diag, lf, 0.0)
            log_hits.append(z + lf)
            totals.append(jnp.sum(lf, axis=-1, keepdims=True))
            parts.extend(_split_bf16(lf))
        within_all = jnp.dot(jnp.concatenate(parts, axis=0), tri, preferred_element_type=F32)
        out = []
        for g, cols in enumerate(heads):
            carry, acc = state[g]
            rows = 2 * g * ATT_T
            within = within_all[rows:rows + ATT_T] + within_all[rows + ATT_T:rows + 2 * ATT_T]
            a = jnp.exp(log_hits[g] + (within + carry))
            if diagonal:
                a = jnp.where(below_diag, a, 0.0)
            vblk = v_ref[pl.ds(start, ATT_T), cols].astype(BF16)
            acc = acc + jnp.dot(a.astype(BF16), vblk, preferred_element_type=F32)
            out.append((carry + totals[g], acc))
        return tuple(out)

    state = tuple((jnp.zeros((ATT_T, 1), F32), jnp.zeros((ATT_T, HEAD_DIM), F32)) for _ in heads)
    state = block(qi, state, True)
    state = lax.fori_loop(0, qi, lambda i, s: block(qi - 1 - i, s, False), state)
    for g, cols in enumerate(heads):
        o_ref[:, cols] = _rmsnorm(state[g][1], gh_ref[g:g + 1, :]).astype(BF16)


def _strict_upper_ones(n):
    j = lax.broadcasted_iota(jnp.int32, (n, n), 0)
    s = lax.broadcasted_iota(jnp.int32, (n, n), 1)
    return (j > s).astype(BF16)


def _attn_prompt(q, k_all, v_all, sb_bias_l, g_head_l, *, layer, batch, seq):
    nq = seq // ATT_T
    gw = ATT_G * HEAD_DIM
    return pl.pallas_call(
        _attn_kernel,
        grid=(batch, N_HEADS // ATT_G, nq),
        in_specs=[
            pl.BlockSpec(memory_space=pltpu.SMEM),
            pl.BlockSpec((ATT_T, gw), lambda b, h, i: (b * nq + i, h)),
            pl.BlockSpec((None, seq, gw), lambda b, h, i: (layer, b, h)),
            pl.BlockSpec((None, seq, gw), lambda b, h, i: (layer, b, h)),
            pl.BlockSpec((ATT_T, ATT_T), lambda b, h, i: (0, 0)),
            pl.BlockSpec((None, ATT_G, HEAD_DIM), lambda b, h, i: (h, 0, 0)),
        ],
        out_specs=pl.BlockSpec((ATT_T, gw), lambda b, h, i: (b * nq + i, h)),
        out_shape=jax.ShapeDtypeStruct((batch * seq, ATT_W), BF16),
        compiler_params=_params(("parallel", "parallel", "arbitrary")),
        name="attn_prompt",
    )(sb_bias_l, q, k_all, v_all, _strict_upper_ones(ATT_T),
      g_head_l.reshape(N_HEADS // ATT_G, ATT_G, HEAD_DIM))


def _attn_sample_kernel(pt_ref, bias_ref, q_ref, *refs, n_pages):
    del pt_ref
    k_refs = refs[:n_pages]
    v_refs = refs[n_pages:2 * n_pages]
    tri_ref, gh_ref, o_ref = refs[2 * n_pages:]
    width = PAGE_SIZE * N_HEADS
    q = q_ref[...].astype(BF16)
    bias = bias_ref[...]
    lane = lax.broadcasted_iota(jnp.int32, (N_HEADS, width), 1)
    sub = lax.broadcasted_iota(jnp.int32, (N_HEADS, width), 0)
    own = jnp.bitwise_and(lane, N_HEADS - 1) == sub

    zs, lfs = [], []
    for p in range(n_pages):
        z = lax.dot_general(q, k_refs[p][...].astype(BF16), (((1,), (1,)), ((), ())),
                            preferred_element_type=F32)
        z = z * ATT_SCALE + bias
        zs.append(z)
        lfs.append(jnp.where(own, -_softplus(z), 0.0))

    lf_all = jnp.concatenate(lfs, axis=0)
    hi, lo = _split_bf16(lf_all)
    tri = tri_ref[...]
    within = (jnp.dot(hi, tri, preferred_element_type=F32)
              + jnp.dot(lo, tri, preferred_element_type=F32))
    totals = jnp.sum(lf_all, axis=-1, keepdims=True)

    carry = jnp.zeros((N_HEADS, 1), F32)
    acc = jnp.zeros((N_HEADS, HEAD_DIM), F32)
    for p in reversed(range(n_pages)):
        rows = slice(p * N_HEADS, (p + 1) * N_HEADS)
        a = jnp.exp(zs[p] + lfs[p] + (within[rows] + carry))
        a = jnp.where(own, a, 0.0).astype(BF16)
        acc = acc + jnp.dot(a, v_refs[p][...].astype(BF16), preferred_element_type=F32)
        carry = carry + totals[rows]
    o_ref[...] = _rmsnorm(acc, gh_ref[...])


def _attn_sample(q_s, cache_k, cache_v, page_table, sb_bias_l, g_head_l, *, layer):
    dec_b, n_pages = page_table.shape
    n_phys = cache_k.shape[1]
    width = PAGE_SIZE * N_HEADS
    ck = cache_k.reshape(cache_k.shape[0], n_phys, width, HEAD_DIM)
    cv = cache_v.reshape(cache_v.shape[0], n_phys, width, HEAD_DIM)

    def page_spec(p):
        return pl.BlockSpec((None, None, width, HEAD_DIM), lambda b, pt, p=p: (layer, pt[b, p], 0, 0))

    in_specs = [
        pl.BlockSpec((N_HEADS, 1), lambda b, pt: (0, 0)),
        pl.BlockSpec((None, N_HEADS, HEAD_DIM), lambda b, pt: (b, 0, 0)),
    ]
    in_specs += [page_spec(p) for p in range(n_pages)]
    in_specs += [page_spec(p) for p in range(n_pages)]
    in_specs += [
        pl.BlockSpec((width, width), lambda b, pt: (0, 0)),
        pl.BlockSpec((N_HEADS, HEAD_DIM), lambda b, pt: (0, 0)),
    ]
    grid_spec = pltpu.PrefetchScalarGridSpec(
        num_scalar_prefetch=1,
        grid=(dec_b,),
        in_specs=in_specs,
        out_specs=pl.BlockSpec((None, N_HEADS, HEAD_DIM), lambda b, pt: (b, 0, 0)),
    )
    return pl.pallas_call(
        functools.partial(_attn_sample_kernel, n_pages=n_pages),
        grid_spec=grid_spec,
        out_shape=jax.ShapeDtypeStruct((dec_b, N_HEADS, HEAD_DIM), F32),
        compiler_params=_params(("arbitrary",)),
        name="attn_sample",
    )(page_table, sb_bias_l.reshape(N_HEADS, 1), q_s, *([ck] * n_pages), *([cv] * n_pages),
      _strict_upper_ones(width), g_head_l)


def _pool_to_cat(diffs, wp_ref, ps_ref, sb_ref, cat_scr):
    for g in range(len(POOL_WINDOWS)):
        cols = slice(g * POOL_GC, (g + 1) * POOL_GC)
        mixed = jnp.dot(diffs[g].astype(BF16), wp_ref[g].astype(BF16), preferred_element_type=F32)
        cat_scr[:, cols] = (mixed * ps_ref[:, cols]).astype(BF16)
    cat_scr[:, POOL_W:] = sb_ref[...].astype(BF16)


def _mixout_prompt_kernel(u_ref, halo_ref, sb_ref, x_ref, gt_ref, wp_ref, ps_ref, wo_ref, o_ref,
                          ext_scr, cat_scr, *, tm, tiles_per_seq):
    i = pl.program_id(0)

    @pl.when(pl.program_id(1) == 0)
    def _():
        first = (i % tiles_per_seq) == 0
        ext_scr[0:HALO, :] = jnp.where(first, 0.0, halo_ref[...])
        ext_scr[HALO:, :] = u_ref[...]
        pos = (i % tiles_per_seq) * tm + lax.broadcasted_iota(jnp.int32, (tm, 1), 0)
        diffs = []
        for g, w in enumerate(POOL_WINDOWS):
            cols = slice(g * POOL_GC, (g + 1) * POOL_GC)
            total = ext_scr[HALO:HALO + tm, cols]
            for d in range(1, w):
                total = total + ext_scr[HALO - d:HALO - d + tm, cols]
            cnt = jnp.minimum(pos + 1, w).astype(F32)
            diffs.append(total / cnt - ext_scr[HALO:HALO + tm, cols])
        _pool_to_cat(diffs, wp_ref, ps_ref, sb_ref, cat_scr)

    width = o_ref.shape[1]
    for c0 in range(0, width, OUT_TN):
        cols = slice(c0, min(c0 + OUT_TN, width))
        mixed = jnp.dot(cat_scr[...], wo_ref[:, cols].astype(BF16), preferred_element_type=F32)
        o_ref[:, cols] = x_ref[:, cols] + gt_ref[:, cols] * mixed


def _mixout_sample_kernel(ue_ref, sb_ref, x_ref, gt_ref, wp_ref, ps_ref, wo_ref, o_ref, cat_scr):
    @pl.when(pl.program_id(1) == 0)
    def _():
        diffs = []
        for g, w in enumerate(POOL_WINDOWS):
            cols = slice(g * POOL_GC, (g + 1) * POOL_GC)
            total = ue_ref[POOL_BUF, :, cols]
            for d in range(1, w):
                total = total + ue_ref[POOL_BUF - d, :, cols]
            diffs.append(total / float(w) - ue_ref[POOL_BUF, :, cols])
        _pool_to_cat(diffs, wp_ref, ps_ref, sb_ref, cat_scr)

    mixed = jnp.dot(cat_scr[...], wo_ref[...].astype(BF16), preferred_element_type=F32)
    o_ref[...] = x_ref[...] + gt_ref[...] * mixed


def _mixout_prompt(u, sb, x, mod, w_pool, pool_scale3, w_out, *, layer, tm, tiles_per_seq):
    n_tok = x.shape[0]
    halo_blocks = tm // HALO
    in_specs = [
        pl.BlockSpec((tm, POOL_W), lambda i, j: (i, 0)),
        pl.BlockSpec((HALO, POOL_W), lambda i, j: (jnp.maximum(i * halo_blocks - 1, 0), 0)),
        pl.BlockSpec((tm, ATT_W), lambda i, j: (i, 0)),
        pl.BlockSpec((tm, D_MODEL), lambda i, j: (i, 0)),
    ]
    in_specs += _mod_specs(False, tm, layer, (5,), tiles_per_seq, D_MODEL, lambda i, j: 0)
    in_specs += [
        pl.BlockSpec((None, len(POOL_WINDOWS), POOL_GC, POOL_GC), lambda i, j: (layer, 0, 0, 0)),
        pl.BlockSpec((None, 1, POOL_W), lambda i, j: (layer, 0, 0)),
        pl.BlockSpec((None, D_MODEL, D_MODEL), lambda i, j: (layer, 0, 0), pipeline_mode=pl.Buffered(1)),
    ]
    return pl.pallas_call(
        functools.partial(_mixout_prompt_kernel, tm=tm, tiles_per_seq=tiles_per_seq),
        grid=(n_tok // tm, 1),
        in_specs=in_specs,
        out_specs=pl.BlockSpec((tm, D_MODEL), lambda i, j: (i, 0)),
        out_shape=jax.ShapeDtypeStruct((n_tok, D_MODEL), F32),
        scratch_shapes=[pltpu.VMEM((tm + HALO, POOL_W), F32), pltpu.VMEM((tm, D_MODEL), BF16)],
        compiler_params=_params(("parallel", "arbitrary")),
        name="mixout_prompt",
    )(u, u, sb, x, mod, w_pool, pool_scale3, w_out)


def _mixout_sample(u_ext, sb, x, mod, w_pool, pool_scale3, w_out, *, layer):
    n_tok = x.shape[0]
    in_specs = [
        pl.BlockSpec((POOL_BUF + 1, n_tok, POOL_W), lambda i, j: (0, 0, 0)),
        pl.BlockSpec((n_tok, ATT_W), lambda i, j: (0, 0)),
        pl.BlockSpec((n_tok, OUT_TN), lambda i, j: (0, j)),
    ]
    in_specs += _mod_specs(True, n_tok, layer, (5,), 1, OUT_TN, lambda i, j: j)
    in_specs += [
        pl.BlockSpec((None, len(POOL_WINDOWS), POOL_GC, POOL_GC), lambda i, j: (layer, 0, 0, 0)),
        pl.BlockSpec((None, 1, POOL_W), lambda i, j: (layer, 0, 0)),
        pl.BlockSpec((None, D_MODEL, OUT_TN), lambda i, j: (layer, 0, j)),
    ]
    return pl.pallas_call(
        _mixout_sample_kernel,
        grid=(1, D_MODEL // OUT_TN),
        in_specs=in_specs,
        out_specs=pl.BlockSpec((n_tok, OUT_TN), lambda i, j: (0, j)),
        out_shape=jax.ShapeDtypeStruct((n_tok, D_MODEL), F32),
        scratch_shapes=[pltpu.VMEM((n_tok, D_MODEL), BF16)],
        compiler_params=_params(("arbitrary", "arbitrary")),
        name="mixout_sample",
    )(u_ext, sb, x, mod, w_pool, pool_scale3, w_out)


def kernel(x_prompt, x_sample, c_prompt, c_sample, cache_k, cache_v, state_pool, page_table,
           w_ada, b_ada, g_norm, w1, w3, w2, w_in, w_pool, pool_scale, g_head, sb_bias, w_out, g_final):
    batch, seq, _ = x_prompt.shape
    dec_b = x_sample.shape[0]
    depth = w_ada.shape[0]
    tm_p = TM_PROMPT
    tiles_per_seq = seq // tm_p

    assert dec_b % 16 == 0
    pad = (-batch) % 16
    c_all = jnp.concatenate([c_sample, c_prompt, jnp.zeros((pad, D_MODEL), F32)], axis=0)
    mod_s, mod_p = _ada(c_all, dec_b, w_ada, b_ada)
    mod_p = mod_p[:, :, :batch].reshape(depth, N_MOD, batch, 1, D_MODEL)

    g_norm4 = g_norm.reshape(depth, 3, 1, D_MODEL)
    pool_scale3 = pool_scale.reshape(depth, 1, POOL_W)
    g_final2 = g_final.reshape(1, D_MODEL)

    xp = x_prompt.reshape(batch * seq, D_MODEL)
    xs = x_sample.reshape(dec_b, D_MODEL)
    kv_p = kv_s = None
    pp_l, ps_l = [], []
    for l in range(depth):
        last = l == depth - 1
        ffn = functools.partial(_ffn, mod_p=mod_p, mod_s=mod_s, g_norm4=g_norm4, w1=w1, w3=w3, w2=w2,
                                g_final=g_final2, layer=l, tm=tm_p, tiles_per_seq=tiles_per_seq)
        xp, xs = ffn(xp, xs, which=0, final_norm=False)
        u_p, q_p, k_p, v_p = _proj(xp, mod_p, g_norm4, w_in, kv_p, layer=l, depth=depth,
                                   per_token=False, tm=tm_p, tiles_per_seq=tiles_per_seq)
        kv_p = (k_p, v_p)
        sb_p = _attn_prompt(q_p, k_p, v_p, sb_bias[l], g_head[l], layer=l, batch=batch, seq=seq)
        xp = _mixout_prompt(u_p, sb_p, xp, mod_p, w_pool, pool_scale3, w_out,
                            layer=l, tm=TM_MIX, tiles_per_seq=seq // TM_MIX)
        pp_l.append(u_p.reshape(batch, seq, POOL_W)[:, seq - POOL_BUF:])

        u_s, q_s, k_s, v_s = _proj(xs, mod_s, g_norm4, w_in, kv_s, layer=l, depth=depth,
                                   per_token=True, tm=dec_b, tiles_per_seq=1)
        kv_s = (k_s, v_s)
        q_s3 = q_s.astype(F32).reshape(dec_b, N_HEADS, HEAD_DIM)
        sb_s = _attn_sample(q_s3, cache_k, cache_v, page_table, sb_bias[l], g_head[l], layer=l)
        u_ext = jnp.concatenate([jnp.transpose(state_pool[l], (1, 0, 2)), u_s[None]], axis=0)
        xs = _mixout_sample(u_ext, sb_s.reshape(dec_b, ATT_W), xs, mod_s, w_pool, pool_scale3, w_out, layer=l)
        ps_l.append(jnp.concatenate([state_pool[l][:, 1:], u_s[:, None, :]], axis=1))

        xp, xs = ffn(xp, xs, which=1, final_norm=last)

    y_prompt = xp.reshape(batch, seq, D_MODEL)
    y_sample = xs.reshape(dec_b, 1, D_MODEL)
    kv5 = lambda a, rows, t: a.reshape(depth, rows, t, N_HEADS, HEAD_DIM)
    return (y_prompt, y_sample, kv5(kv_p[0], batch, seq), kv5(kv_p[1], batch, seq), jnp.stack(pp_l),
            kv5(kv_s[0], dec_b, 1), kv5(kv_s[1], dec_b, 1), jnp.stack(ps_l))
```

```python
import functools

import jax
import jax.numpy as jnp
from jax import lax
from jax.experimental import pallas as pl
from jax.experimental.pallas import tpu as pltpu

D_MODEL = 2048
POOL_W = 1024
ATT_W = 1024
HEAD_DIM = 128
N_HEADS = 8
POOL_WINDOWS = (2, 4, 8, 16)
POOL_GC = 256
POOL_BUF = 15
D_FF = 5632
N_MOD = 9
EPS = 1e-6
PAGE_SIZE = 128
PROJ_W = POOL_W + 3 * ATT_W
ATT_SCALE = HEAD_DIM ** -0.5

F32 = jnp.float32
BF16 = jnp.bfloat16

VMEM_LIMIT_BYTES = 56 * 1024 * 1024

TM_PROMPT = 1024
TM_MIX = 512
FFN_TF = 256
PROJ_TN = 512
OUT_TN = 512
ADA_TN = 1024
ATT_T = 256
ATT_G = 8
ROW_CHUNK = 128
HALO = 16


def _params(sem):
    return pltpu.CompilerParams(dimension_semantics=sem, vmem_limit_bytes=VMEM_LIMIT_BYTES)


def _modulate(x, g, shift, scale):
    ms = jnp.mean(x * x, axis=-1, keepdims=True)
    y = x * lax.rsqrt(ms + EPS) * g
    return y * (1.0 + scale) + shift


def _rmsnorm(x, g):
    ms = jnp.mean(x * x, axis=-1, keepdims=True)
    return x * lax.rsqrt(ms + EPS) * g


def _softplus(z):
    return jnp.maximum(z, 0.0) + jnp.log(1.0 + jnp.exp(-jnp.abs(z)))


def _split_bf16(x):
    hi = x.astype(BF16)
    lo = (x - hi.astype(F32)).astype(BF16)
    return hi, lo


def _ada_kernel(c_ref, w_ref, b_ref, os_ref, op_ref):
    s = jax.nn.silu(c_ref[...]).astype(BF16)
    mod = jnp.dot(s, w_ref[...].astype(BF16), preferred_element_type=F32) + b_ref[...]
    n_s = os_ref.shape[0]
    os_ref[...] = mod[:n_s]
    op_ref[...] = mod[n_s:]


def _ada(c_all, n_sample, w_ada, b_ada):
    depth = w_ada.shape[0]
    rows = c_all.shape[0]
    per_chunk = D_MODEL // ADA_TN
    out_map = lambda l, n: (l, n // per_chunk, 0, n % per_chunk)
    return pl.pallas_call(
        _ada_kernel,
        grid=(depth, N_MOD * per_chunk),
        in_specs=[
            pl.BlockSpec((rows, D_MODEL), lambda l, n: (0, 0)),
            pl.BlockSpec((None, D_MODEL, ADA_TN), lambda l, n: (l, 0, n)),
            pl.BlockSpec((None, 1, ADA_TN), lambda l, n: (l, 0, n)),
        ],
        out_specs=[pl.BlockSpec((None, None, n_sample, ADA_TN), out_map),
                   pl.BlockSpec((None, None, rows - n_sample, ADA_TN), out_map)],
        out_shape=[jax.ShapeDtypeStruct((depth, N_MOD, n_sample, D_MODEL), F32),
                   jax.ShapeDtypeStruct((depth, N_MOD, rows - n_sample, D_MODEL), F32)],
        compiler_params=_params(("parallel", "arbitrary")),
        name="ada",
    )(c_all, w_ada, b_ada.reshape(depth, 1, N_MOD * D_MODEL))


def _mod_specs(per_token, tm, layer, chunks, tiles_per_seq, width, col_of):
    specs = []
    for chunk in chunks:
        if per_token:
            specs.append(pl.BlockSpec((None, None, tm, width),
                                      lambda i, j, c=chunk: (layer, c, i, col_of(i, j))))
        else:
            specs.append(pl.BlockSpec((None, None, None, 1, width),
                                      lambda i, j, c=chunk: (layer, c, i // tiles_per_seq, 0, col_of(i, j))))
    return specs


def _modulate_rows(x_ref, g_ref, sh_ref, sc_ref, h_scr, per_token):
    tm = x_ref.shape[0]
    chunk = min(ROW_CHUNK, tm)

    def body(c, carry):
        rows = pl.ds(pl.multiple_of(c * chunk, chunk), chunk)
        sh = sh_ref[rows, :] if per_token else sh_ref[...]
        sc = sc_ref[rows, :] if per_token else sc_ref[...]
        h_scr[rows, :] = _modulate(x_ref[rows, :], g_ref[...], sh, sc).astype(BF16)
        return carry

    lax.fori_loop(0, tm // chunk, body, 0)


def _ffn_kernel(xp_ref, xs_ref, shp_ref, scp_ref, gtp_ref, shs_ref, scs_ref, gts_ref, g_ref,
                w1_ref, w3_ref, w2_ref, gf_ref, op_ref, os_ref, h_scr, act_scr, acc_scr, *, final_norm):
    j = pl.program_id(1)
    last = pl.num_programs(1) - 1
    tm = xp_ref.shape[0]

    def up():
        h = h_scr[...]
        a = jnp.dot(h, w1_ref[...].astype(BF16), preferred_element_type=F32)
        b = jnp.dot(h, w3_ref[...].astype(BF16), preferred_element_type=F32)
        return a, b

    def down():
        return jnp.dot(act_scr[...], w2_ref[...].astype(BF16), preferred_element_type=F32)

    @pl.when(j == 0)
    def _():
        _modulate_rows(xp_ref, g_ref, shp_ref, scp_ref, h_scr, False)
        h_scr[tm:, :] = _modulate(xs_ref[...], g_ref[...], shs_ref[...], scs_ref[...]).astype(BF16)
        acc_scr[...] = jnp.zeros_like(acc_scr)
        a, b = up()
        act_scr[...] = (jax.nn.silu(a) * b).astype(BF16)

    @pl.when(jnp.logical_and(j > 0, j < last))
    def _():
        a, b = up()
        acc_scr[...] += down()
        act_scr[...] = (jax.nn.silu(a) * b).astype(BF16)

    @pl.when(j == last)
    def _():
        acc_scr[...] += down()
        yp = xp_ref[...] + 0.5 * gtp_ref[...] * acc_scr[:tm, :]
        ys = xs_ref[...] + 0.5 * gts_ref[...] * acc_scr[tm:, :]
        if final_norm:
            yp = _rmsnorm(yp, gf_ref[...])
            ys = _rmsnorm(ys, gf_ref[...])
        op_ref[...] = yp
        os_ref[...] = ys


def _ffn(xp, xs, mod_p, mod_s, g_norm4, w1, w3, w2, g_final, *, layer, which, tm, tiles_per_seq, final_norm):
    n_tok = xp.shape[0]
    n_tiles = n_tok // tm
    rs = xs.shape[0] // n_tiles
    assert rs * n_tiles == xs.shape[0] and rs % 8 == 0
    n_ff = D_FF // FFN_TF
    chunk0 = 0 if which == 0 else 6
    norm_idx = 0 if which == 0 else 2
    chunks = (chunk0, chunk0 + 1, chunk0 + 2)
    zero_col = lambda i, j: 0
    in_specs = [
        pl.BlockSpec((tm, D_MODEL), lambda i, j: (i, 0), pipeline_mode=pl.Buffered(1)),
        pl.BlockSpec((rs, D_MODEL), lambda i, j: (i, 0)),
    ]
    in_specs += _mod_specs(False, tm, layer, chunks, tiles_per_seq, D_MODEL, zero_col)
    in_specs += _mod_specs(True, rs, layer, chunks, 1, D_MODEL, zero_col)
    in_specs += [
        pl.BlockSpec((None, None, 1, D_MODEL), lambda i, j: (layer, norm_idx, 0, 0)),
        pl.BlockSpec((None, None, D_MODEL, FFN_TF), lambda i, j: (layer, which, 0, jnp.minimum(j, n_ff - 1))),
        pl.BlockSpec((None, None, D_MODEL, FFN_TF), lambda i, j: (layer, which, 0, jnp.minimum(j, n_ff - 1))),
        pl.BlockSpec((None, None, FFN_TF, D_MODEL), lambda i, j: (layer, which, jnp.maximum(j - 1, 0), 0)),
        pl.BlockSpec((1, D_MODEL), lambda i, j: (0, 0)),
    ]
    return pl.pallas_call(
        functools.partial(_ffn_kernel, final_norm=final_norm),
        grid=(n_tiles, n_ff + 1),
        in_specs=in_specs,
        out_specs=[pl.BlockSpec((tm, D_MODEL), lambda i, j: (i, 0)),
                   pl.BlockSpec((rs, D_MODEL), lambda i, j: (i, 0))],
        out_shape=[jax.ShapeDtypeStruct((n_tok, D_MODEL), F32),
                   jax.ShapeDtypeStruct(xs.shape, F32)],
        scratch_shapes=[pltpu.VMEM((tm + rs, D_MODEL), BF16), pltpu.VMEM((tm + rs, FFN_TF), BF16),
                        pltpu.VMEM((tm + rs, D_MODEL), F32)],
        compiler_params=_params(("parallel", "arbitrary")),
        name="ffn",
    )(xp, xs, mod_p, mod_p, mod_p, mod_s, mod_s, mod_s, g_norm4, w1, w3, w2, g_final)


_PROJ_HALVES = POOL_W // PROJ_TN


def _proj_kernel(*refs, per_token):
    x_ref, sh_ref, sc_ref, g_ref, w_ref = refs[:5]
    u_ref, q_ref, k_ref, v_ref, h_scr = refs[-5:]
    j = pl.program_id(1)

    @pl.when(j == 0)
    def _():
        _modulate_rows(x_ref, g_ref, sh_ref, sc_ref, h_scr, per_token)

    for idx, ref in enumerate((u_ref, q_ref, k_ref, v_ref)):
        @pl.when(j // _PROJ_HALVES == idx)
        def _(ref=ref):
            res = jnp.dot(h_scr[...], w_ref[...].astype(BF16), preferred_element_type=F32)
            ref[...] = res.astype(ref.dtype)


def _proj(x, mod, g_norm4, w_in, kv_prev, *, layer, depth, per_token, tm, tiles_per_seq):
    n_tok = x.shape[0]
    zero_col = lambda i, j: 0

    def half_of(idx):
        return lambda i, j: jnp.clip(j - idx * _PROJ_HALVES, 0, _PROJ_HALVES - 1)

    in_specs = [pl.BlockSpec((tm, D_MODEL), lambda i, j: (i, 0))]
    in_specs += _mod_specs(per_token, tm, layer, (3, 4), tiles_per_seq, D_MODEL, zero_col)
    in_specs += [
        pl.BlockSpec((None, None, 1, D_MODEL), lambda i, j: (layer, 1, 0, 0)),
        pl.BlockSpec((None, D_MODEL, PROJ_TN), lambda i, j: (layer, 0, j)),
    ]
    operands = [x, mod, mod, g_norm4, w_in]
    aliases = {}
    if kv_prev is not None:
        in_specs += [pl.BlockSpec(memory_space=pl.ANY), pl.BlockSpec(memory_space=pl.ANY)]
        aliases = {len(operands): 2, len(operands) + 1: 3}
        operands += list(kv_prev)
    out_specs = [
        pl.BlockSpec((tm, PROJ_TN), lambda i, j: (i, half_of(0)(i, j))),
        pl.BlockSpec((tm, PROJ_TN), lambda i, j: (i, half_of(1)(i, j))),
        pl.BlockSpec((None, tm, PROJ_TN), lambda i, j: (layer, i, half_of(2)(i, j))),
        pl.BlockSpec((None, tm, PROJ_TN), lambda i, j: (layer, i, half_of(3)(i, j))),
    ]
    out_shape = [
        jax.ShapeDtypeStruct((n_tok, POOL_W), F32),
        jax.ShapeDtypeStruct((n_tok, ATT_W), BF16),
        jax.ShapeDtypeStruct((depth, n_tok, ATT_W), F32),
        jax.ShapeDtypeStruct((depth, n_tok, ATT_W), F32),
    ]
    return pl.pallas_call(
        functools.partial(_proj_kernel, per_token=per_token),
        grid=(n_tok // tm, PROJ_W // PROJ_TN),
        in_specs=in_specs,
        out_specs=out_specs,
        out_shape=out_shape,
        input_output_aliases=aliases,
        scratch_shapes=[pltpu.VMEM((tm, D_MODEL), BF16)],
        compiler_params=_params(("parallel", "arbitrary")),
        name="proj",
    )(*operands)


def _attn_kernel(bias_ref, q_ref, k_ref, v_ref, tri_ref, gh_ref, o_ref, acc_scr, carry_scr):
    group = pl.program_id(1)
    qi = pl.program_id(2)
    tri = tri_ref[...]
    row = lax.broadcasted_iota(jnp.int32, (ATT_T, ATT_T), 0)
    col = lax.broadcasted_iota(jnp.int32, (ATT_T, ATT_T), 1)
    below_diag = col < row
    heads = [slice(g * HEAD_DIM, (g + 1) * HEAD_DIM) for g in range(ATT_G)]
    qs = [q_ref[:, cols] for cols in heads]
    biases = [bias_ref[group * ATT_G + g] for g in range(ATT_G)]
    half = ATT_G // 2
    head_sets = (tuple(range(half)), tuple(range(half, ATT_G)))

    def block(kb, diagonal):
        start = pl.multiple_of(kb * ATT_T, ATT_T)
        zs = []
        for g, cols in enumerate(heads):
            kblk = k_ref[pl.ds(start, ATT_T), cols].astype(BF16)
            z = lax.dot_general(qs[g], kblk, (((1,), (1,)), ((), ())), preferred_element_type=F32)
            zs.append(z * ATT_SCALE + biases[g])

        def fail_logs(head_set):
            log_hits, totals, parts = [], [], []
            for g in head_set:
                lf = -_softplus(zs[g])
                if diagonal:
                    lf = jnp.where(below_diag, lf, 0.0)
                log_hits.append(zs[g] + lf)
                totals.append(jnp.sum(lf, axis=-1, keepdims=True))
                parts.extend(_split_bf16(lf))
            sums = jnp.dot(jnp.concatenate(parts, axis=0), tri, preferred_element_type=F32)
            return log_hits, totals, sums

        def weigh(head_set, log_hits, totals, sums):
            for n, g in enumerate(head_set):
                rows = 2 * n * ATT_T
                within = sums[rows:rows + ATT_T] + sums[rows + ATT_T:rows + 2 * ATT_T]
                carry = carry_scr[g]
                between = within + jnp.concatenate([carry] * (ATT_T // HEAD_DIM), axis=1)
                a = jnp.exp(log_hits[n] + between)
                if diagonal:
                    a = jnp.where(below_diag, a, 0.0)
                vblk = v_ref[pl.ds(start, ATT_T), heads[g]].astype(BF16)
                acc_scr[g] += jnp.dot(a.astype(BF16), vblk, preferred_element_type=F32)
                carry_scr[g] = carry + jnp.broadcast_to(totals[n], carry.shape)

        first = fail_logs(head_sets[0])
        second = fail_logs(head_sets[1])
        weigh(head_sets[0], *first)
        weigh(head_sets[1], *second)

    acc_scr[...] = jnp.zeros_like(acc_scr)
    carry_scr[...] = jnp.zeros_like(carry_scr)
    block(qi, True)

    def body(i, c):
        block(qi - 1 - i, False)
        return c

    lax.fori_loop(0, qi, body, 0)
    for g, cols in enumerate(heads):
        o_ref[:, cols] = _rmsnorm(acc_scr[g], gh_ref[g:g + 1, :]).astype(BF16)


def _strict_upper_ones(n):
    j = lax.broadcasted_iota(jnp.int32, (n, n), 0)
    s = lax.broadcasted_iota(jnp.int32, (n, n), 1)
    return (j > s).astype(BF16)


def _attn_prompt(q, k_all, v_all, sb_bias_l, g_head_l, *, layer, batch, seq):
    nq = seq // ATT_T
    gw = ATT_G * HEAD_DIM
    return pl.pallas_call(
        _attn_kernel,
        grid=(batch, N_HEADS // ATT_G, nq),
        in_specs=[
            pl.BlockSpec(memory_space=pltpu.SMEM),
            pl.BlockSpec((ATT_T, gw), lambda b, h, i: (b * nq + i, h)),
            pl.BlockSpec((None, seq, gw), lambda b, h, i: (layer, b, h)),
            pl.BlockSpec((None, seq, gw), lambda b, h, i: (layer, b, h)),
            pl.BlockSpec((ATT_T, ATT_T), lambda b, h, i: (0, 0)),
            pl.BlockSpec((None, ATT_G, HEAD_DIM), lambda b, h, i: (h, 0, 0)),
        ],
        out_specs=pl.BlockSpec((ATT_T, gw), lambda b, h, i: (b * nq + i, h)),
        out_shape=jax.ShapeDtypeStruct((batch * seq, ATT_W), BF16),
        scratch_shapes=[pltpu.VMEM((ATT_G, ATT_T, HEAD_DIM), F32), pltpu.VMEM((ATT_G, ATT_T, HEAD_DIM), F32)],
        compiler_params=_params(("parallel", "parallel", "arbitrary")),
        name="attn_prompt",
    )(sb_bias_l, q, k_all, v_all, _strict_upper_ones(ATT_T),
      g_head_l.reshape(N_HEADS // ATT_G, ATT_G, HEAD_DIM))


def _attn_sample_kernel(pt_ref, bias_ref, q_ref, *refs, n_pages):
    del pt_ref
    k_refs = refs[:n_pages]
    v_refs = refs[n_pages:2 * n_pages]
    tri_ref, gh_ref, o_ref = refs[2 * n_pages:]
    width = PAGE_SIZE * N_HEADS
    q = q_ref[...].astype(BF16)
    bias = bias_ref[...]
    lane = lax.broadcasted_iota(jnp.int32, (N_HEADS, width), 1)
    sub = lax.broadcasted_iota(jnp.int32, (N_HEADS, width), 0)
    own = jnp.bitwise_and(lane, N_HEADS - 1) == sub

    zs, lfs = [], []
    for p in range(n_pages):
        z = lax.dot_general(q, k_refs[p][...].astype(BF16), (((1,), (1,)), ((), ())),
                            preferred_element_type=F32)
        z = z * ATT_SCALE + bias
        zs.append(z)
        lfs.append(jnp.where(own, -_softplus(z), 0.0))

    lf_all = jnp.concatenate(lfs, axis=0)
    hi, lo = _split_bf16(lf_all)
    tri = tri_ref[...]
    within = (jnp.dot(hi, tri, preferred_element_type=F32)
              + jnp.dot(lo, tri, preferred_element_type=F32))
    totals = jnp.sum(lf_all, axis=-1, keepdims=True)

    carry = jnp.zeros((N_HEADS, 1), F32)
    acc = jnp.zeros((N_HEADS, HEAD_DIM), F32)
    for p in reversed(range(n_pages)):
        rows = slice(p * N_HEADS, (p + 1) * N_HEADS)
        a = jnp.exp(zs[p] + lfs[p] + (within[rows] + carry))
        a = jnp.where(own, a, 0.0).astype(BF16)
        acc = acc + jnp.dot(a, v_refs[p][...].astype(BF16), preferred_element_type=F32)
        carry = carry + totals[rows]
    o_ref[...] = _rmsnorm(acc, gh_ref[...])


def _attn_sample(q_s, cache_k, cache_v, page_table, sb_bias_l, g_head_l, *, layer):
    dec_b, n_pages = page_table.shape
    n_phys = cache_k.shape[1]
    width = PAGE_SIZE * N_HEADS
    ck = cache_k.reshape(cache_k.shape[0], n_phys, width, HEAD_DIM)
    cv = cache_v.reshape(cache_v.shape[0], n_phys, width, HEAD_DIM)

    def page_spec(p):
        return pl.BlockSpec((None, None, width, HEAD_DIM), lambda b, pt, p=p: (layer, pt[b, p], 0, 0))

    in_specs = [
        pl.BlockSpec((N_HEADS, 1), lambda b, pt: (0, 0)),
        pl.BlockSpec((None, N_HEADS, HEAD_DIM), lambda b, pt: (b, 0, 0)),
    ]
    in_specs += [page_spec(p) for p in range(n_pages)]
    in_specs += [page_spec(p) for p in range(n_pages)]
    in_specs += [
        pl.BlockSpec((width, width), lambda b, pt: (0, 0)),
        pl.BlockSpec((N_HEADS, HEAD_DIM), lambda b, pt: (0, 0)),
    ]
    grid_spec = pltpu.PrefetchScalarGridSpec(
        num_scalar_prefetch=1,
        grid=(dec_b,),
        in_specs=in_specs,
        out_specs=pl.BlockSpec((None, N_HEADS, HEAD_DIM), lambda b, pt: (b, 0, 0)),
    )
    return pl.pallas_call(
        functools.partial(_attn_sample_kernel, n_pages=n_pages),
        grid_spec=grid_spec,
        out_shape=jax.ShapeDtypeStruct((dec_b, N_HEADS, HEAD_DIM), F32),
        compiler_params=_params(("arbitrary",)),
        name="attn_sample",
    )(page_table, sb_bias_l.reshape(N_HEADS, 1), q_s, *([ck] * n_pages), *([cv] * n_pages),
      _strict_upper_ones(width), g_head_l)


def _pool_to_cat(diffs, wp_ref, ps_ref, sb_ref, cat_scr):
    for g in range(len(POOL_WINDOWS)):
        cols = slice(g * POOL_GC, (g + 1) * POOL_GC)
        mixed = jnp.dot(diffs[g].astype(BF16), wp_ref[g].astype(BF16), preferred_element_type=F32)
        cat_scr[:, cols] = (mixed * ps_ref[:, cols]).astype(BF16)
    cat_scr[:, POOL_W:] = sb_ref[...].astype(BF16)


def _mixout_prompt_kernel(u_ref, halo_ref, sb_ref, x_ref, gt_ref, wp_ref, ps_ref, wo_ref, o_ref,
                          ext_scr, cat_scr, *, tm, tiles_per_seq):
    i = pl.program_id(0)

    @pl.when(pl.program_id(1) == 0)
    def _():
        first = (i % tiles_per_seq) == 0
        ext_scr[0:HALO, :] = jnp.where(first, 0.0, halo_ref[...])
        ext_scr[HALO:, :] = u_ref[...]
        pos = (i % tiles_per_seq) * tm + lax.broadcasted_iota(jnp.int32, (tm, 1), 0)
        diffs = []
        for g, w in enumerate(POOL_WINDOWS):
            cols = slice(g * POOL_GC, (g + 1) * POOL_GC)
            total = ext_scr[HALO:HALO + tm, cols]
            for d in range(1, w):
                total = total + ext_scr[HALO - d:HALO - d + tm, cols]
            cnt = jnp.minimum(pos + 1, w).astype(F32)
            diffs.append(total / cnt - ext_scr[HALO:HALO + tm, cols])
        _pool_to_cat(diffs, wp_ref, ps_ref, sb_ref, cat_scr)

    width = o_ref.shape[1]
    for c0 in range(0, width, OUT_TN):
        cols = slice(c0, min(c0 + OUT_TN, width))
        mixed = jnp.dot(cat_scr[...], wo_ref[:, cols].astype(BF16), preferred_element_type=F32)
        o_ref[:, cols] = x_ref[:, cols] + gt_ref[:, cols] * mixed


def _mixout_sample_kernel(ue_ref, sb_ref, x_ref, gt_ref, wp_ref, ps_ref, wo_ref, o_ref, cat_scr):
    @pl.when(pl.program_id(1) == 0)
    def _():
        diffs = []
        for g, w in enumerate(POOL_WINDOWS):
            cols = slice(g * POOL_GC, (g + 1) * POOL_GC)
            total = ue_ref[POOL_BUF, :, cols]
            for d in range(1, w):
                total = total + ue_ref[POOL_BUF - d, :, cols]
            diffs.append(total / float(w) - ue_ref[POOL_BUF, :, cols])
        _pool_to_cat(diffs, wp_ref, ps_ref, sb_ref, cat_scr)

    mixed = jnp.dot(cat_scr[...], wo_ref[...].astype(BF16), preferred_element_type=F32)
    o_ref[...] = x_ref[...] + gt_ref[...] * mixed


def _mixout_prompt(u, sb, x, mod, w_pool, pool_scale3, w_out, *, layer, tm, tiles_per_seq):
    n_tok = x.shape[0]
    halo_blocks = tm // HALO
    in_specs = [
        pl.BlockSpec((tm, POOL_W), lambda i, j: (i, 0)),
        pl.BlockSpec((HALO, POOL_W), lambda i, j: (jnp.maximum(i * halo_blocks - 1, 0), 0)),
        pl.BlockSpec((tm, ATT_W), lambda i, j: (i, 0)),
        pl.BlockSpec((tm, D_MODEL), lambda i, j: (i, 0)),
    ]
    in_specs += _mod_specs(False, tm, layer, (5,), tiles_per_seq, D_MODEL, lambda i, j: 0)
    in_specs += [
        pl.BlockSpec((None, len(POOL_WINDOWS), POOL_GC, POOL_GC), lambda i, j: (layer, 0, 0, 0)),
        pl.BlockSpec((None, 1, POOL_W), lambda i, j: (layer, 0, 0)),
        pl.BlockSpec((None, D_MODEL, D_MODEL), lambda i, j: (layer, 0, 0), pipeline_mode=pl.Buffered(1)),
    ]
    return pl.pallas_call(
        functools.partial(_mixout_prompt_kernel, tm=tm, tiles_per_seq=tiles_per_seq),
        grid=(n_tok // tm, 1),
        in_specs=in_specs,
        out_specs=pl.BlockSpec((tm, D_MODEL), lambda i, j: (i, 0)),
        out_shape=jax.ShapeDtypeStruct((n_tok, D_MODEL), F32),
        scratch_shapes=[pltpu.VMEM((tm + HALO, POOL_W), F32), pltpu.VMEM((tm, D_MODEL), BF16)],
        compiler_params=_params(("parallel", "arbitrary")),
        name="mixout_prompt",
    )(u, u, sb, x, mod, w_pool, pool_scale3, w_out)


def _mixout_sample(u_ext, sb, x, mod, w_pool, pool_scale3, w_out, *, layer):
    n_tok = x.shape[0]
    in_specs = [
        pl.BlockSpec((POOL_BUF + 1, n_tok, POOL_W), lambda i, j: (0, 0, 0)),
        pl.BlockSpec((n_tok, ATT_W), lambda i, j: (0, 0)),
        pl.BlockSpec((n_tok, OUT_TN), lambda i, j: (0, j)),
    ]
    in_specs += _mod_specs(True, n_tok, layer, (5,), 1, OUT_TN, lambda i, j: j)
    in_specs += [
        pl.BlockSpec((None, len(POOL_WINDOWS), POOL_GC, POOL_GC), lambda i, j: (layer, 0, 0, 0)),
        pl.BlockSpec((None, 1, POOL_W), lambda i, j: (layer, 0, 0)),
        pl.BlockSpec((None, D_MODEL, OUT_TN), lambda i, j: (layer, 0, j)),
    ]
    return pl.pallas_call(
        _mixout_sample_kernel,
        grid=(1, D_MODEL // OUT_TN),
        in_specs=in_specs,
        out_specs=pl.BlockSpec((n_tok, OUT_TN), lambda i, j: (0, j)),
        out_shape=jax.ShapeDtypeStruct((n_tok, D_MODEL), F32),
        scratch_shapes=[pltpu.VMEM((n_tok, D_MODEL), BF16)],
        compiler_params=_params(("arbitrary", "arbitrary")),
        name="mixout_sample",
    )(u_ext, sb, x, mod, w_pool, pool_scale3, w_out)


def kernel(x_prompt, x_sample, c_prompt, c_sample, cache_k, cache_v, state_pool, page_table,
           w_ada, b_ada, g_norm, w1, w3, w2, w_in, w_pool, pool_scale, g_head, sb_bias, w_out, g_final):
    batch, seq, _ = x_prompt.shape
    dec_b = x_sample.shape[0]
    depth = w_ada.shape[0]
    tm_p = TM_PROMPT
    tiles_per_seq = seq // tm_p

    assert dec_b % 16 == 0
    pad = (-batch) % 16
    c_all = jnp.concatenate([c_sample, c_prompt, jnp.zeros((pad, D_MODEL), F32)], axis=0)
    mod_s, mod_p = _ada(c_all, dec_b, w_ada, b_ada)
    mod_p = mod_p[:, :, :batch].reshape(depth, N_MOD, batch, 1, D_MODEL)

    g_norm4 = g_norm.reshape(depth, 3, 1, D_MODEL)
    pool_scale3 = pool_scale.reshape(depth, 1, POOL_W)
    g_final2 = g_final.reshape(1, D_MODEL)

    xp = x_prompt.reshape(batch * seq, D_MODEL)
    xs = x_sample.reshape(dec_b, D_MODEL)
    kv_p = kv_s = None
    pp_l, ps_l = [], []
    for l in range(depth):
        last = l == depth - 1
        ffn = functools.partial(_ffn, mod_p=mod_p, mod_s=mod_s, g_norm4=g_norm4, w1=w1, w3=w3, w2=w2,
                                g_final=g_final2, layer=l, tm=tm_p, tiles_per_seq=tiles_per_seq)
        xp, xs = ffn(xp, xs, which=0, final_norm=False)
        u_p, q_p, k_p, v_p = _proj(xp, mod_p, g_norm4, w_in, kv_p, layer=l, depth=depth,
                                   per_token=False, tm=tm_p, tiles_per_seq=tiles_per_seq)
        kv_p = (k_p, v_p)
        sb_p = _attn_prompt(q_p, k_p, v_p, sb_bias[l], g_head[l], layer=l, batch=batch, seq=seq)
        xp = _mixout_prompt(u_p, sb_p, xp, mod_p, w_pool, pool_scale3, w_out,
                            layer=l, tm=TM_MIX, tiles_per_seq=seq // TM_MIX)
        pp_l.append(u_p.reshape(batch, seq, POOL_W)[:, seq - POOL_BUF:])

        u_s, q_s, k_s, v_s = _proj(xs, mod_s, g_norm4, w_in, kv_s, layer=l, depth=depth,
                                   per_token=True, tm=dec_b, tiles_per_seq=1)
        kv_s = (k_s, v_s)
        q_s3 = q_s.astype(F32).reshape(dec_b, N_HEADS, HEAD_DIM)
        sb_s = _attn_sample(q_s3, cache_k, cache_v, page_table, sb_bias[l], g_head[l], layer=l)
        u_ext = jnp.concatenate([jnp.transpose(state_pool[l], (1, 0, 2)), u_s[None]], axis=0)
        xs = _mixout_sample(u_ext, sb_s.reshape(dec_b, ATT_W), xs, mod_s, w_pool, pool_scale3, w_out, layer=l)
        ps_l.append(jnp.concatenate([state_pool[l][:, 1:], u_s[:, None, :]], axis=1))

        xp, xs = ffn(xp, xs, which=1, final_norm=last)

    y_prompt = xp.reshape(batch, seq, D_MODEL)
    y_sample = xs.reshape(dec_b, 1, D_MODEL)
    kv5 = lambda a, rows, t: a.reshape(depth, rows, t, N_HEADS, HEAD_DIM)
    return (y_prompt, y_sample, kv5(kv_p[0], batch, seq), kv5(kv_p[1], batch, seq), jnp.stack(pp_l),
            kv5(kv_s[0], dec_b, 1), kv5(kv_s[1], dec_b, 1), jnp.stack(ps_l))
```

```python
import functools

import jax
import jax.numpy as jnp
from jax import lax
from jax.experimental import pallas as pl
from jax.experimental.pallas import tpu as pltpu

D_MODEL = 2048
POOL_W = 1024
ATT_W = 1024
HEAD_DIM = 128
N_HEADS = 8
POOL_WINDOWS = (2, 4, 8, 16)
POOL_GC = 256
POOL_BUF = 15
D_FF = 5632
N_MOD = 9
EPS = 1e-6
PAGE_SIZE = 128
PROJ_W = POOL_W + 3 * ATT_W
ATT_SCALE = HEAD_DIM ** -0.5

F32 = jnp.float32
BF16 = jnp.bfloat16

VMEM_LIMIT_BYTES = 56 * 1024 * 1024

TM_PROMPT = 1024
TM_MIX = 512
FFN_TF = 256
PROJ_TN = 512
OUT_TN = 512
ADA_TN = 1024
ATT_T = 256
ATT_G = 8
ROW_CHUNK = 128
HALO = 16


def _params(sem):
    return pltpu.CompilerParams(dimension_semantics=sem, vmem_limit_bytes=VMEM_LIMIT_BYTES)


def _modulate(x, g, shift, scale):
    ms = jnp.mean(x * x, axis=-1, keepdims=True)
    y = x * lax.rsqrt(ms + EPS) * g
    return y * (1.0 + scale) + shift


def _rmsnorm(x, g):
    ms = jnp.mean(x * x, axis=-1, keepdims=True)
    return x * lax.rsqrt(ms + EPS) * g


def _softplus(z):
    return jnp.maximum(z, 0.0) + jnp.log(1.0 + jnp.exp(-jnp.abs(z)))


def _split_bf16(x):
    hi = x.astype(BF16)
    lo = (x - hi.astype(F32)).astype(BF16)
    return hi, lo


def _ada_kernel(c_ref, w_ref, b_ref, os_ref, op_ref):
    s = jax.nn.silu(c_ref[...]).astype(BF16)
    mod = jnp.dot(s, w_ref[...].astype(BF16), preferred_element_type=F32) + b_ref[...]
    n_s = os_ref.shape[0]
    os_ref[...] = mod[:n_s]
    op_ref[...] = mod[n_s:]


def _ada(c_all, n_sample, w_ada, b_ada):
    depth = w_ada.shape[0]
    rows = c_all.shape[0]
    per_chunk = D_MODEL // ADA_TN
    out_map = lambda l, n: (l, n // per_chunk, 0, n % per_chunk)
    return pl.pallas_call(
        _ada_kernel,
        grid=(depth, N_MOD * per_chunk),
        in_specs=[
            pl.BlockSpec((rows, D_MODEL), lambda l, n: (0, 0)),
            pl.BlockSpec((None, D_MODEL, ADA_TN), lambda l, n: (l, 0, n)),
            pl.BlockSpec((None, 1, ADA_TN), lambda l, n: (l, 0, n)),
        ],
        out_specs=[pl.BlockSpec((None, None, n_sample, ADA_TN), out_map),
                   pl.BlockSpec((None, None, rows - n_sample, ADA_TN), out_map)],
        out_shape=[jax.ShapeDtypeStruct((depth, N_MOD, n_sample, D_MODEL), F32),
                   jax.ShapeDtypeStruct((depth, N_MOD, rows - n_sample, D_MODEL), F32)],
        compiler_params=_params(("parallel", "arbitrary")),
        name="ada",
    )(c_all, w_ada, b_ada.reshape(depth, 1, N_MOD * D_MODEL))


def _mod_specs(per_token, tm, layer, chunks, tiles_per_seq, width, col_of):
    specs = []
    for chunk in chunks:
        if per_token:
            specs.append(pl.BlockSpec((None, None, tm, width),
                                      lambda i, j, c=chunk: (layer, c, i, col_of(i, j))))
        else:
            specs.append(pl.BlockSpec((None, None, None, 1, width),
                                      lambda i, j, c=chunk: (layer, c, i // tiles_per_seq, 0, col_of(i, j))))
    return specs


def _modulate_tile(xp_ref, xs_ref, g_ref, shp_ref, scp_ref, shs_ref, scs_ref, h_scr):
    tm = xp_ref.shape[0]
    chunk = min(ROW_CHUNK, tm)

    def body(c, carry):
        rows = pl.ds(pl.multiple_of(c * chunk, chunk), chunk)
        h_scr[rows, :] = _modulate(xp_ref[rows, :], g_ref[...], shp_ref[...], scp_ref[...]).astype(BF16)
        return carry

    lax.fori_loop(0, tm // chunk, body, 0)
    h_scr[tm:, :] = _modulate(xs_ref[...], g_ref[...], shs_ref[...], scs_ref[...]).astype(BF16)


def _ffn_kernel(xp_ref, xs_ref, shp_ref, scp_ref, gtp_ref, shs_ref, scs_ref, gts_ref, g_ref,
                w1_ref, w3_ref, w2_ref, gf_ref, op_ref, os_ref, h_scr, act_scr, acc_scr, *, final_norm):
    j = pl.program_id(1)
    last = pl.num_programs(1) - 1
    tm = xp_ref.shape[0]

    def up():
        h = h_scr[...]
        a = jnp.dot(h, w1_ref[...].astype(BF16), preferred_element_type=F32)
        b = jnp.dot(h, w3_ref[...].astype(BF16), preferred_element_type=F32)
        return a, b

    def down():
        return jnp.dot(act_scr[...], w2_ref[...].astype(BF16), preferred_element_type=F32)

    @pl.when(j == 0)
    def _():
        _modulate_tile(xp_ref, xs_ref, g_ref, shp_ref, scp_ref, shs_ref, scs_ref, h_scr)
        acc_scr[...] = jnp.zeros_like(acc_scr)
        a, b = up()
        act_scr[...] = (jax.nn.silu(a) * b).astype(BF16)

    @pl.when(jnp.logical_and(j > 0, j < last))
    def _():
        a, b = up()
        acc_scr[...] += down()
        act_scr[...] = (jax.nn.silu(a) * b).astype(BF16)

    @pl.when(j == last)
    def _():
        acc_scr[...] += down()
        yp = xp_ref[...] + 0.5 * gtp_ref[...] * acc_scr[:tm, :]
        ys = xs_ref[...] + 0.5 * gts_ref[...] * acc_scr[tm:, :]
        if final_norm:
            yp = _rmsnorm(yp, gf_ref[...])
            ys = _rmsnorm(ys, gf_ref[...])
        op_ref[...] = yp
        os_ref[...] = ys


def _ffn(xp, xs, mod_p, mod_s, g_norm4, w1, w3, w2, g_final, *, layer, which, tm, tiles_per_seq, final_norm):
    n_tok = xp.shape[0]
    n_tiles = n_tok // tm
    rs = xs.shape[0] // n_tiles
    assert rs * n_tiles == xs.shape[0] and rs % 8 == 0
    n_ff = D_FF // FFN_TF
    chunk0 = 0 if which == 0 else 6
    norm_idx = 0 if which == 0 else 2
    chunks = (chunk0, chunk0 + 1, chunk0 + 2)
    zero_col = lambda i, j: 0
    in_specs = [
        pl.BlockSpec((tm, D_MODEL), lambda i, j: (i, 0), pipeline_mode=pl.Buffered(1)),
        pl.BlockSpec((rs, D_MODEL), lambda i, j: (i, 0)),
    ]
    in_specs += _mod_specs(False, tm, layer, chunks, tiles_per_seq, D_MODEL, zero_col)
    in_specs += _mod_specs(True, rs, layer, chunks, 1, D_MODEL, zero_col)
    in_specs += [
        pl.BlockSpec((None, None, 1, D_MODEL), lambda i, j: (layer, norm_idx, 0, 0)),
        pl.BlockSpec((None, None, D_MODEL, FFN_TF), lambda i, j: (layer, which, 0, jnp.minimum(j, n_ff - 1))),
        pl.BlockSpec((None, None, D_MODEL, FFN_TF), lambda i, j: (layer, which, 0, jnp.minimum(j, n_ff - 1))),
        pl.BlockSpec((None, None, FFN_TF, D_MODEL), lambda i, j: (layer, which, jnp.maximum(j - 1, 0), 0)),
        pl.BlockSpec((1, D_MODEL), lambda i, j: (0, 0)),
    ]
    return pl.pallas_call(
        functools.partial(_ffn_kernel, final_norm=final_norm),
        grid=(n_tiles, n_ff + 1),
        in_specs=in_specs,
        out_specs=[pl.BlockSpec((tm, D_MODEL), lambda i, j: (i, 0)),
                   pl.BlockSpec((rs, D_MODEL), lambda i, j: (i, 0))],
        out_shape=[jax.ShapeDtypeStruct((n_tok, D_MODEL), F32),
                   jax.ShapeDtypeStruct(xs.shape, F32)],
        scratch_shapes=[pltpu.VMEM((tm + rs, D_MODEL), BF16), pltpu.VMEM((tm + rs, FFN_TF), BF16),
                        pltpu.VMEM((tm + rs, D_MODEL), F32)],
        compiler_params=_params(("parallel", "arbitrary")),
        name="ffn",
    )(xp, xs, mod_p, mod_p, mod_p, mod_s, mod_s, mod_s, g_norm4, w1, w3, w2, g_final)


_PROJ_HALVES = POOL_W // PROJ_TN


def _proj_kernel(*refs):
    xp_ref, xs_ref, shp_ref, scp_ref, shs_ref, scs_ref, g_ref, w_ref = refs[:8]
    prompt_outs, sample_outs, h_scr = refs[-9:-5], refs[-5:-1], refs[-1]
    j = pl.program_id(1)
    tm = xp_ref.shape[0]

    @pl.when(j == 0)
    def _():
        _modulate_tile(xp_ref, xs_ref, g_ref, shp_ref, scp_ref, shs_ref, scs_ref, h_scr)

    for idx, (p_ref, s_ref) in enumerate(zip(prompt_outs, sample_outs)):
        @pl.when(j // _PROJ_HALVES == idx)
        def _(p_ref=p_ref, s_ref=s_ref):
            res = jnp.dot(h_scr[...], w_ref[...].astype(BF16), preferred_element_type=F32)
            p_ref[...] = res[:tm].astype(p_ref.dtype)
            s_ref[...] = res[tm:].astype(s_ref.dtype)


def _proj(xp, xs, mod_p, mod_s, g_norm4, w_in, kv_prev, *, layer, depth, tm, tiles_per_seq):
    n_tok = xp.shape[0]
    n_tiles = n_tok // tm
    rs = xs.shape[0] // n_tiles
    assert rs * n_tiles == xs.shape[0] and rs % 16 == 0
    zero_col = lambda i, j: 0

    def half_of(idx):
        return lambda i, j: jnp.clip(j - idx * _PROJ_HALVES, 0, _PROJ_HALVES - 1)

    in_specs = [pl.BlockSpec((tm, D_MODEL), lambda i, j: (i, 0)),
                pl.BlockSpec((rs, D_MODEL), lambda i, j: (i, 0))]
    in_specs += _mod_specs(False, tm, layer, (3, 4), tiles_per_seq, D_MODEL, zero_col)
    in_specs += _mod_specs(True, rs, layer, (3, 4), 1, D_MODEL, zero_col)
    in_specs += [
        pl.BlockSpec((None, None, 1, D_MODEL), lambda i, j: (layer, 1, 0, 0)),
        pl.BlockSpec((None, D_MODEL, PROJ_TN), lambda i, j: (layer, 0, j)),
    ]
    operands = [xp, xs, mod_p, mod_p, mod_s, mod_s, g_norm4, w_in]
    aliases = {}
    if kv_prev is not None:
        in_specs += [pl.BlockSpec(memory_space=pl.ANY)] * 4
        first = len(operands)
        aliases = {first: 2, first + 1: 3, first + 2: 6, first + 3: 7}
        operands += list(kv_prev)

    def group_specs(rows):
        return [
            pl.BlockSpec((rows, PROJ_TN), lambda i, j: (i, half_of(0)(i, j))),
            pl.BlockSpec((rows, PROJ_TN), lambda i, j: (i, half_of(1)(i, j))),
            pl.BlockSpec((None, rows, PROJ_TN), lambda i, j: (layer, i, half_of(2)(i, j))),
            pl.BlockSpec((None, rows, PROJ_TN), lambda i, j: (layer, i, half_of(3)(i, j))),
        ]

    def group_shapes(n):
        return [
            jax.ShapeDtypeStruct((n, POOL_W), F32),
            jax.ShapeDtypeStruct((n, ATT_W), BF16),
            jax.ShapeDtypeStruct((depth, n, ATT_W), F32),
            jax.ShapeDtypeStruct((depth, n, ATT_W), F32),
        ]

    return pl.pallas_call(
        _proj_kernel,
        grid=(n_tiles, PROJ_W // PROJ_TN),
        in_specs=in_specs,
        out_specs=group_specs(tm) + group_specs(rs),
        out_shape=group_shapes(n_tok) + group_shapes(xs.shape[0]),
        input_output_aliases=aliases,
        scratch_shapes=[pltpu.VMEM((tm + rs, D_MODEL), BF16)],
        compiler_params=_params(("parallel", "arbitrary")),
        name="proj",
    )(*operands)


def _attn_kernel(bias_ref, q_ref, k_ref, v_ref, tri_ref, gh_ref, o_ref, acc_scr, carry_scr):
    group = pl.program_id(1)
    qi = pl.program_id(2)
    tri = tri_ref[...]
    row = lax.broadcasted_iota(jnp.int32, (ATT_T, ATT_T), 0)
    col = lax.broadcasted_iota(jnp.int32, (ATT_T, ATT_T), 1)
    below_diag = col < row
    heads = [slice(g * HEAD_DIM, (g + 1) * HEAD_DIM) for g in range(ATT_G)]
    qs = [q_ref[:, cols] for cols in heads]
    biases = [bias_ref[group * ATT_G + g] for g in range(ATT_G)]
    half = ATT_G // 2
    head_sets = (tuple(range(half)), tuple(range(half, ATT_G)))

    def block(kb, diagonal):
        start = pl.multiple_of(kb * ATT_T, ATT_T)
        zs = []
        for g, cols in enumerate(heads):
            kblk = k_ref[pl.ds(start, ATT_T), cols].astype(BF16)
            z = lax.dot_general(qs[g], kblk, (((1,), (1,)), ((), ())), preferred_element_type=F32)
            zs.append(z * ATT_SCALE + biases[g])

        def fail_logs(head_set):
            log_hits, totals, parts = [], [], []
            for g in head_set:
                lf = -_softplus(zs[g])
                if diagonal:
                    lf = jnp.where(below_diag, lf, 0.0)
                log_hits.append(zs[g] + lf)
                totals.append(jnp.sum(lf, axis=-1, keepdims=True))
                parts.extend(_split_bf16(lf))
            sums = jnp.dot(jnp.concatenate(parts, axis=0), tri, preferred_element_type=F32)
            return log_hits, totals, sums

        def weigh(head_set, log_hits, totals, sums):
            for n, g in enumerate(head_set):
                rows = 2 * n * ATT_T
                within = sums[rows:rows + ATT_T] + sums[rows + ATT_T:rows + 2 * ATT_T]
                carry = carry_scr[g]
                between = within + jnp.concatenate([carry] * (ATT_T // HEAD_DIM), axis=1)
                a = jnp.exp(log_hits[n] + between)
                if diagonal:
                    a = jnp.where(below_diag, a, 0.0)
                vblk = v_ref[pl.ds(start, ATT_T), heads[g]].astype(BF16)
                acc_scr[g] += jnp.dot(a.astype(BF16), vblk, preferred_element_type=F32)
                carry_scr[g] = carry + jnp.broadcast_to(totals[n], carry.shape)

        first = fail_logs(head_sets[0])
        second = fail_logs(head_sets[1])
        weigh(head_sets[0], *first)
        weigh(head_sets[1], *second)

    acc_scr[...] = jnp.zeros_like(acc_scr)
    carry_scr[...] = jnp.zeros_like(carry_scr)
    block(qi, True)

    def body(i, c):
        block(qi - 1 - i, False)
        return c

    lax.fori_loop(0, qi, body, 0)
    for g, cols in enumerate(heads):
        o_ref[:, cols] = _rmsnorm(acc_scr[g], gh_ref[g:g + 1, :]).astype(BF16)


def _strict_upper_ones(n):
    j = lax.broadcasted_iota(jnp.int32, (n, n), 0)
    s = lax.broadcasted_iota(jnp.int32, (n, n), 1)
    return (j > s).astype(BF16)


def _attn_prompt(q, k_all, v_all, sb_bias_l, g_head_l, *, layer, batch, seq):
    nq = seq // ATT_T
    gw = ATT_G * HEAD_DIM
    return pl.pallas_call(
        _attn_kernel,
        grid=(batch, N_HEADS // ATT_G, nq),
        in_specs=[
            pl.BlockSpec(memory_space=pltpu.SMEM),
            pl.BlockSpec((ATT_T, gw), lambda b, h, i: (b * nq + i, h)),
            pl.BlockSpec((None, seq, gw), lambda b, h, i: (layer, b, h)),
            pl.BlockSpec((None, seq, gw), lambda b, h, i: (layer, b, h)),
            pl.BlockSpec((ATT_T, ATT_T), lambda b, h, i: (0, 0)),
            pl.BlockSpec((None, ATT_G, HEAD_DIM), lambda b, h, i: (h, 0, 0)),
        ],
        out_specs=pl.BlockSpec((ATT_T, gw), lambda b, h, i: (b * nq + i, h)),
        out_shape=jax.ShapeDtypeStruct((batch * seq, ATT_W), BF16),
        scratch_shapes=[pltpu.VMEM((ATT_G, ATT_T, HEAD_DIM), F32), pltpu.VMEM((ATT_G, ATT_T, HEAD_DIM), F32)],
        compiler_params=_params(("parallel", "parallel", "arbitrary")),
        name="attn_prompt",
    )(sb_bias_l, q, k_all, v_all, _strict_upper_ones(ATT_T),
      g_head_l.reshape(N_HEADS // ATT_G, ATT_G, HEAD_DIM))


def _attn_sample_kernel(pt_ref, bias_ref, q_ref, *refs, n_pages):
    del pt_ref
    k_refs = refs[:n_pages]
    v_refs = refs[n_pages:2 * n_pages]
    tri_ref, gh_ref, o_ref = refs[2 * n_pages:]
    width = PAGE_SIZE * N_HEADS
    q = q_ref[...].astype(BF16)
    bias = bias_ref[...]
    lane = lax.broadcasted_iota(jnp.int32, (N_HEADS, width), 1)
    sub = lax.broadcasted_iota(jnp.int32, (N_HEADS, width), 0)
    own = jnp.bitwise_and(lane, N_HEADS - 1) == sub

    zs, lfs = [], []
    for p in range(n_pages):
        z = lax.dot_general(q, k_refs[p][...].astype(BF16), (((1,), (1,)), ((), ())),
                            preferred_element_type=F32)
        z = z * ATT_SCALE + bias
        zs.append(z)
        lfs.append(jnp.where(own, -_softplus(z), 0.0))

    lf_all = jnp.concatenate(lfs, axis=0)
    hi, lo = _split_bf16(lf_all)
    tri = tri_ref[...]
    within = (jnp.dot(hi, tri, preferred_element_type=F32)
              + jnp.dot(lo, tri, preferred_element_type=F32))
    totals = jnp.sum(lf_all, axis=-1, keepdims=True)

    carry = jnp.zeros((N_HEADS, 1), F32)
    acc = jnp.zeros((N_HEADS, HEAD_DIM), F32)
    for p in reversed(range(n_pages)):
        rows = slice(p * N_HEADS, (p + 1) * N_HEADS)
        a = jnp.exp(zs[p] + lfs[p] + (within[rows] + carry))
        a = jnp.where(own, a, 0.0).astype(BF16)
        acc = acc + jnp.dot(a, v_refs[p][...].astype(BF16), preferred_element_type=F32)
        carry = carry + totals[rows]
    o_ref[...] = _rmsnorm(acc, gh_ref[...])


def _attn_sample(q_s, cache_k, cache_v, page_table, sb_bias_l, g_head_l, *, layer):
    dec_b, n_pages = page_table.shape
    n_phys = cache_k.shape[1]
    width = PAGE_SIZE * N_HEADS
    ck = cache_k.reshape(cache_k.shape[0], n_phys, width, HEAD_DIM)
    cv = cache_v.reshape(cache_v.shape[0], n_phys, width, HEAD_DIM)

    def page_spec(p):
        return pl.BlockSpec((None, None, width, HEAD_DIM), lambda b, pt, p=p: (layer, pt[b, p], 0, 0))

    in_specs = [
        pl.BlockSpec((N_HEADS, 1), lambda b, pt: (0, 0)),
        pl.BlockSpec((None, N_HEADS, HEAD_DIM), lambda b, pt: (b, 0, 0)),
    ]
    in_specs += [page_spec(p) for p in range(n_pages)]
    in_specs += [page_spec(p) for p in range(n_pages)]
    in_specs += [
        pl.BlockSpec((width, width), lambda b, pt: (0, 0)),
        pl.BlockSpec((N_HEADS, HEAD_DIM), lambda b, pt: (0, 0)),
    ]
    grid_spec = pltpu.PrefetchScalarGridSpec(
        num_scalar_prefetch=1,
        grid=(dec_b,),
        in_specs=in_specs,
        out_specs=pl.BlockSpec((None, N_HEADS, HEAD_DIM), lambda b, pt: (b, 0, 0)),
    )
    return pl.pallas_call(
        functools.partial(_attn_sample_kernel, n_pages=n_pages),
        grid_spec=grid_spec,
        out_shape=jax.ShapeDtypeStruct((dec_b, N_HEADS, HEAD_DIM), F32),
        compiler_params=_params(("arbitrary",)),
        name="attn_sample",
    )(page_table, sb_bias_l.reshape(N_HEADS, 1), q_s, *([ck] * n_pages), *([cv] * n_pages),
      _strict_upper_ones(width), g_head_l)


def _pool_to_cat(diffs, wp_ref, ps_ref, sb_ref, cat_scr):
    for g in range(len(POOL_WINDOWS)):
        cols = slice(g * POOL_GC, (g + 1) * POOL_GC)
        mixed = jnp.dot(diffs[g].astype(BF16), wp_ref[g].astype(BF16), preferred_element_type=F32)
        cat_scr[:, cols] = (mixed * ps_ref[:, cols]).astype(BF16)
    cat_scr[:, POOL_W:] = sb_ref[...].astype(BF16)


def _mixout_prompt_kernel(u_ref, halo_ref, sb_ref, x_ref, gt_ref, wp_ref, ps_ref, wo_ref, o_ref,
                          ext_scr, cat_scr, *, tm, tiles_per_seq):
    i = pl.program_id(0)

    @pl.when(pl.program_id(1) == 0)
    def _():
        first = (i % tiles_per_seq) == 0
        ext_scr[0:HALO, :] = jnp.where(first, 0.0, halo_ref[...])
        ext_scr[HALO:, :] = u_ref[...]
        pos = (i % tiles_per_seq) * tm + lax.broadcasted_iota(jnp.int32, (tm, 1), 0)
        diffs = []
        for g, w in enumerate(POOL_WINDOWS):
            cols = slice(g * POOL_GC, (g + 1) * POOL_GC)
            total = ext_scr[HALO:HALO + tm, cols]
            for d in range(1, w):
                total = total + ext_scr[HALO - d:HALO - d + tm, cols]
            cnt = jnp.minimum(pos + 1, w).astype(F32)
            diffs.append(total / cnt - ext_scr[HALO:HALO + tm, cols])
        _pool_to_cat(diffs, wp_ref, ps_ref, sb_ref, cat_scr)

    width = o_ref.shape[1]
    for c0 in range(0, width, OUT_TN):
        cols = slice(c0, min(c0 + OUT_TN, width))
        mixed = jnp.dot(cat_scr[...], wo_ref[:, cols].astype(BF16), preferred_element_type=F32)
        o_ref[:, cols] = x_ref[:, cols] + gt_ref[:, cols] * mixed


def _mixout_sample_kernel(ue_ref, sb_ref, x_ref, gt_ref, wp_ref, ps_ref, wo_ref, o_ref, cat_scr):
    @pl.when(pl.program_id(1) == 0)
    def _():
        diffs = []
        for g, w in enumerate(POOL_WINDOWS):
            cols = slice(g * POOL_GC, (g + 1) * POOL_GC)
            total = ue_ref[POOL_BUF, :, cols]
            for d in range(1, w):
                total = total + ue_ref[POOL_BUF - d, :, cols]
            diffs.append(total / float(w) - ue_ref[POOL_BUF, :, cols])
        _pool_to_cat(diffs, wp_ref, ps_ref, sb_ref, cat_scr)

    mixed = jnp.dot(cat_scr[...], wo_ref[...].astype(BF16), preferred_element_type=F32)
    o_ref[...] = x_ref[...] + gt_ref[...] * mixed


def _mixout_prompt(u, sb, x, mod, w_pool, pool_scale3, w_out, *, layer, tm, tiles_per_seq):
    n_tok = x.shape[0]
    halo_blocks = tm // HALO
    in_specs = [
        pl.BlockSpec((tm, POOL_W), lambda i, j: (i, 0)),
        pl.BlockSpec((HALO, POOL_W), lambda i, j: (jnp.maximum(i * halo_blocks - 1, 0), 0)),
        pl.BlockSpec((tm, ATT_W), lambda i, j: (i, 0)),
        pl.BlockSpec((tm, D_MODEL), lambda i, j: (i, 0)),
    ]
    in_specs += _mod_specs(False, tm, layer, (5,), tiles_per_seq, D_MODEL, lambda i, j: 0)
    in_specs += [
        pl.BlockSpec((None, len(POOL_WINDOWS), POOL_GC, POOL_GC), lambda i, j: (layer, 0, 0, 0)),
        pl.BlockSpec((None, 1, POOL_W), lambda i, j: (layer, 0, 0)),
        pl.BlockSpec((None, D_MODEL, D_MODEL), lambda i, j: (layer, 0, 0), pipeline_mode=pl.Buffered(1)),
    ]
    return pl.pallas_call(
        functools.partial(_mixout_prompt_kernel, tm=tm, tiles_per_seq=tiles_per_seq),
        grid=(n_tok // tm, 1),
        in_specs=in_specs,
        out_specs=pl.BlockSpec((tm, D_MODEL), lambda i, j: (i, 0)),
        out_shape=jax.ShapeDtypeStruct((n_tok, D_MODEL), F32),
        scratch_shapes=[pltpu.VMEM((tm + HALO, POOL_W), F32), pltpu.VMEM((tm, D_MODEL), BF16)],
        compiler_params=_params(("parallel", "arbitrary")),
        name="mixout_prompt",
    )(u, u, sb, x, mod, w_pool, pool_scale3, w_out)


def _mixout_sample(u_ext, sb, x, mod, w_pool, pool_scale3, w_out, *, layer):
    n_tok = x.shape[0]
    in_specs = [
        pl.BlockSpec((POOL_BUF + 1, n_tok, POOL_W), lambda i, j: (0, 0, 0)),
        pl.BlockSpec((n_tok, ATT_W), lambda i, j: (0, 0)),
        pl.BlockSpec((n_tok, OUT_TN), lambda i, j: (0, j)),
    ]
    in_specs += _mod_specs(True, n_tok, layer, (5,), 1, OUT_TN, lambda i, j: j)
    in_specs += [
        pl.BlockSpec((None, len(POOL_WINDOWS), POOL_GC, POOL_GC), lambda i, j: (layer, 0, 0, 0)),
        pl.BlockSpec((None, 1, POOL_W), lambda i, j: (layer, 0, 0)),
        pl.BlockSpec((None, D_MODEL, OUT_TN), lambda i, j: (layer, 0, j)),
    ]
    return pl.pallas_call(
        _mixout_sample_kernel,
        grid=(1, D_MODEL // OUT_TN),
        in_specs=in_specs,
        out_specs=pl.BlockSpec((n_tok, OUT_TN), lambda i, j: (0, j)),
        out_shape=jax.ShapeDtypeStruct((n_tok, D_MODEL), F32),
        scratch_shapes=[pltpu.VMEM((n_tok, D_MODEL), BF16)],
        compiler_params=_params(("arbitrary", "arbitrary")),
        name="mixout_sample",
    )(u_ext, sb, x, mod, w_pool, pool_scale3, w_out)


def kernel(x_prompt, x_sample, c_prompt, c_sample, cache_k, cache_v, state_pool, page_table,
           w_ada, b_ada, g_norm, w1, w3, w2, w_in, w_pool, pool_scale, g_head, sb_bias, w_out, g_final):
    batch, seq, _ = x_prompt.shape
    dec_b = x_sample.shape[0]
    depth = w_ada.shape[0]
    tm_p = TM_PROMPT
    tiles_per_seq = seq // tm_p

    assert dec_b % 16 == 0
    pad = (-batch) % 16
    c_all = jnp.concatenate([c_sample, c_prompt, jnp.zeros((pad, D_MODEL), F32)], axis=0)
    mod_s, mod_p = _ada(c_all, dec_b, w_ada, b_ada)
    mod_p = mod_p[:, :, :batch].reshape(depth, N_MOD, batch, 1, D_MODEL)

    g_norm4 = g_norm.reshape(depth, 3, 1, D_MODEL)
    pool_scale3 = pool_scale.reshape(depth, 1, POOL_W)
    g_final2 = g_final.reshape(1, D_MODEL)

    xp = x_prompt.reshape(batch * seq, D_MODEL)
    xs = x_sample.reshape(dec_b, D_MODEL)
    kv_all = None
    pp_l, ps_l = [], []
    for l in range(depth):
        last = l == depth - 1
        ffn = functools.partial(_ffn, mod_p=mod_p, mod_s=mod_s, g_norm4=g_norm4, w1=w1, w3=w3, w2=w2,
                                g_final=g_final2, layer=l, tm=tm_p, tiles_per_seq=tiles_per_seq)
        xp, xs = ffn(xp, xs, which=0, final_norm=False)
        u_p, q_p, k_p, v_p, u_s, q_s, k_s, v_s = _proj(xp, xs, mod_p, mod_s, g_norm4, w_in, kv_all, layer=l,
                                                       depth=depth, tm=tm_p, tiles_per_seq=tiles_per_seq)
        kv_all = (k_p, v_p, k_s, v_s)
        sb_p = _attn_prompt(q_p, k_p, v_p, sb_bias[l], g_head[l], layer=l, batch=batch, seq=seq)
        xp = _mixout_prompt(u_p, sb_p, xp, mod_p, w_pool, pool_scale3, w_out,
                            layer=l, tm=TM_MIX, tiles_per_seq=seq // TM_MIX)
        pp_l.append(u_p.reshape(batch, seq, POOL_W)[:, seq - POOL_BUF:])

        q_s3 =q_s.astype(F32).reshape(dec_b, N_HEADS, HEAD_DIM)
        sb_s = _attn_sample(q_s3, cache_k, cache_v, page_table, sb_bias[l], g_head[l], layer=l)
        u_ext = jnp.concatenate([jnp.transpose(state_pool[l], (1, 0, 2)), u_s[None]], axis=0)
        xs = _mixout_sample(u_ext, sb_s.reshape(dec_b, ATT_W), xs, mod_s, w_pool, pool_scale3, w_out, layer=l)
        ps_l.append(jnp.concatenate([state_pool[l][:, 1:], u_s[:, None, :]], axis=1))

        xp, xs = ffn(xp, xs, which=1, final_norm=last)

    y_prompt = xp.reshape(batch, seq, D_MODEL)
    y_sample = xs.reshape(dec_b, 1, D_MODEL)
    kv5 = lambda a, rows, t: a.reshape(depth, rows, t, N_HEADS, HEAD_DIM)
    k_p, v_p, k_s, v_s = kv_all
    return (y_prompt, y_sample, kv5(k_p, batch, seq), kv5(v_p, batch, seq), jnp.stack(pp_l),
            kv5(k_s, dec_b, 1), kv5(v_s, dec_b, 1), jnp.stack(ps_l))
```
